```python
import math
import jax, jax.numpy as jnp
from jax import lax
import numpy as np

D_MODEL = 1024
BATCH = 8
SEQ = 2048
DEPTH = 4
DEC_BATCH = 32
DEC_SEQ = 2048
PAST_LEN = 128

DN_HEADS = 4
DN_DK = 128
DN_DV = 128
DN_QK_W = DN_HEADS * DN_DK
DN_WIDTH = DN_HEADS * DN_DV
CONV_K = 5
CHUNK = 64
DA_HEADS = 4
DA_DQK = 64
DA_DV = 2 * DA_DQK
DA_QK_W = DA_HEADS * 2 * DA_DQK
DA_WIDTH = DA_HEADS * DA_DV
Q_BLOCK = 128
D_FF = 2816
N_BRANCH = 2
NORM_EPS = 1e-6
SUBLN_EPS = 1e-5
SPLIT_SIZES = (DN_QK_W, DN_QK_W, DN_WIDTH, DN_WIDTH, 2 * DN_HEADS, 2 * DN_HEADS,
               DA_QK_W, DA_QK_W, DA_WIDTH, N_BRANCH * D_MODEL)
N_IN = 2 * DN_QK_W + 2 * DN_WIDTH + 4 * DN_HEADS + 2 * DA_QK_W + DA_WIDTH + N_BRANCH * D_MODEL
CONV_CH = 2 * DN_QK_W + DN_WIDTH

kernel_name = 'hybrid_bidir_deltanet_diffattn_encoder'


def rmsnorm(x, w, eps=NORM_EPS):
    xf = x.astype(jnp.float32)
    y = xf * lax.rsqrt(jnp.mean(xf * xf, axis=-1, keepdims=True) + eps)
    return (y * w.astype(jnp.float32)).astype(x.dtype)


def swiglu(x, wg, wu, wd):
    return (jax.nn.silu(x @ wg) * (x @ wu)) @ wd


def l2norm(x, eps=1e-6):
    return x * lax.rsqrt(jnp.sum(x * x, axis=-1, keepdims=True) + eps)


def short_conv(x, w):
    return lax.conv_general_dilated(
        x, w[:, None, :], window_strides=(1,),
        padding=[(CONV_K // 2, CONV_K // 2)],
        dimension_numbers=('NWC', 'WIO', 'NWC'),
        feature_group_count=x.shape[-1])


def alibi_slopes():
    return 2.0 ** (-8.0 * jnp.arange(1, DA_HEADS + 1, dtype=jnp.float32) / DA_HEADS)


def chunk_gated_delta(q, k, v, beta, g):
    B, S, H, DK = q.shape
    DV = v.shape[-1]
    N = S // CHUNK
    q, k, v = [t.reshape(B, N, CHUNK, H, t.shape[-1]).transpose(0, 3, 1, 2, 4) for t in (q, k, v)]
    beta, g = [t.reshape(B, N, CHUNK, H).transpose(0, 3, 1, 2) for t in (beta, g)]
    gc = jnp.cumsum(g, axis=-1)
    idx = jnp.arange(CHUNK)
    incl = idx[:, None] >= idx[None, :]
    strict = idx[:, None] > idx[None, :]
    decay = jnp.exp(jnp.where(incl, gc[..., :, None] - gc[..., None, :], -jnp.inf))
    kb = k * beta[..., None]
    lower = jnp.where(strict, jnp.einsum('bhncd,bhnsd->bhncs', kb, k) * decay, 0.0)
    eye = jnp.eye(CHUNK, dtype=q.dtype)
    rhs = jnp.concatenate([v * beta[..., None], kb * jnp.exp(gc)[..., None]], axis=-1)
    sol = lax.linalg.triangular_solve(lower + eye, rhs, left_side=True, lower=True,
                                      unit_diagonal=True)
    u, w = sol[..., :DV], sol[..., DV:]
    a_qk = jnp.einsum('bhncd,bhnsd->bhncs', q, k) * decay
    q_g = q * jnp.exp(gc)[..., None]
    g_last = gc[..., -1]
    k_d = k * jnp.exp(g_last[..., None] - gc)[..., None]

    def step(state, xs):
        q_c, a_c, u_c, w_c, k_c, gl_c = xs
        delta = u_c - jnp.einsum('bhcd,bhde->bhce', w_c, state)
        o = jnp.einsum('bhcd,bhde->bhce', q_c, state) + jnp.einsum('bhcs,bhse->bhce', a_c, delta)
        state = state * jnp.exp(gl_c)[..., None, None] + jnp.einsum('bhcd,bhce->bhde', k_c, delta)
        return state, o

    xs = tuple(jnp.moveaxis(t, 2, 0) for t in (q_g, a_qk, u, w, k_d, g_last))
    s0 = jnp.zeros((B, H, DK, DV), q.dtype)
    _, o = lax.scan(step, s0, xs)
    return o.transpose(1, 0, 3, 2, 4).reshape(B, S, H, DV)


def gated_deltanet(q, k, v, z, b, a, conv_w, a_log, dt_bias, norm_w):
    B, S, _ = q.shape
    f32 = jnp.float32
    qkv = jnp.concatenate([q, k, v], axis=-1).astype(f32)
    qkv = jax.nn.silu(short_conv(qkv, conv_w.astype(f32)))
    q, k, v = jnp.split(qkv, [DN_QK_W, 2 * DN_QK_W], axis=-1)
    q = l2norm(q.reshape(B, S, DN_HEADS, DN_DK)) * (DN_DK ** -0.5)
    k = l2norm(k.reshape(B, S, DN_HEADS, DN_DK))
    v = v.reshape(B, S, DN_HEADS, DN_DV)
    beta = jax.nn.sigmoid(b.astype(f32)).reshape(B, S, 2, DN_HEADS)
    gdec = -jnp.exp(a_log.astype(f32)) * jax.nn.softplus(
        a.astype(f32).reshape(B, S, 2, DN_HEADS) + dt_bias.astype(f32))
    o_fwd = chunk_gated_delta(q, k, v, beta[:, :, 0], gdec[:, :, 0])
    flip = lambda t: t[:, ::-1]
    o_bwd = flip(chunk_gated_delta(flip(q), flip(k), flip(v), flip(beta[:, :, 1]), flip(gdec[:, :, 1])))
    o = rmsnorm(o_fwd + o_bwd, norm_w) * jax.nn.silu(z.astype(f32).reshape(B, S, DN_HEADS, DN_DV))
    return o.reshape(B, S, DN_WIDTH)


def diff_attention(q, k, v, lam_params, subln_w, lambda_init):
    B, S, _ = q.shape
    q = q.reshape(B, S, DA_HEADS, 2, DA_DQK) * (DA_DQK ** -0.5)
    k = k.reshape(B, S, DA_HEADS, 2, DA_DQK)
    v = v.reshape(B, S, DA_HEADS, DA_DV)
    lp = lam_params.astype(jnp.float32)
    lam = jnp.exp(jnp.sum(lp[0] * lp[1])) - jnp.exp(jnp.sum(lp[2] * lp[3])) + lambda_init
    slopes = alibi_slopes()
    kpos = jnp.arange(S)
    nb = S // Q_BLOCK
    qb = q.reshape(B, nb, Q_BLOCK, DA_HEADS, 2, DA_DQK).transpose(1, 0, 2, 3, 4, 5)

    def block(args):
        q_blk, start = args
        qpos = start + jnp.arange(Q_BLOCK)
        bias = -slopes[:, None, None] * jnp.abs(qpos[:, None] - kpos[None, :]).astype(jnp.float32)
        s = jnp.einsum('bqhmd,bkhmd->bhmqk', q_blk, k).astype(jnp.float32) + bias[None, :, None]
        p = jax.nn.softmax(s, axis=-1)
        attn = p[:, :, 0] - lam * p[:, :, 1]
        return jnp.einsum('bhqk,bkhe->bqhe', attn.astype(v.dtype), v)

    o = lax.map(block, (qb, jnp.arange(nb) * Q_BLOCK))
    o = o.transpose(1, 0, 2, 3, 4).reshape(B, S, DA_HEADS, DA_DV)
    o = rmsnorm(o, subln_w, eps=SUBLN_EPS) * (1.0 - lambda_init)
    return o.reshape(B, S, DA_WIDTH)


def split_cols(proj):
    idx, acc = [], 0
    for s in SPLIT_SIZES[:-1]:
        acc += s
        idx.append(acc)
    return jnp.split(proj, idx, axis=-1)


def trunk(x, ffn1_norm, ffn1_wg, ffn1_wu, ffn1_wd, mix_norm, w_in, conv_w, dn_a_log,
          dn_dt_bias, dn_out_norm, diff_lambda, diff_subln, w_branch_dn, w_branch_da, w_out,
          ffn2_norm, ffn2_wg, ffn2_wu, ffn2_wd, final_norm):
    B, S, D = x.shape
    for l in range(DEPTH):
        x = x + 0.5 * swiglu(rmsnorm(x, ffn1_norm[l]), ffn1_wg[l], ffn1_wu[l], ffn1_wd[l])
        h = rmsnorm(x, mix_norm[l])
        (dq, dk, dv, dz, db, da, aq, ak, av, gates) = split_cols(h @ w_in[l])
        y_dn = gated_deltanet(dq, dk, dv, dz, db, da, conv_w[l], dn_a_log[l], dn_dt_bias[l],
                              dn_out_norm[l]).astype(x.dtype) @ w_branch_dn[l]
        lambda_init = 0.8 - 0.6 * math.exp(-0.3 * l)
        y_da = diff_attention(aq, ak, av, diff_lambda[l], diff_subln[l], lambda_init).astype(x.dtype) @ w_branch_da[l]
        g = jax.nn.sigmoid(gates.astype(jnp.float32)).reshape(B, S, N_BRANCH, D).astype(x.dtype)
        merged = g[:, :, 0] * y_dn + g[:, :, 1] * y_da
        x = x + merged @ w_out[l]
        x = x + 0.5 * swiglu(rmsnorm(x, ffn2_norm[l]), ffn2_wg[l], ffn2_wu[l], ffn2_wd[l])
    return rmsnorm(x, final_norm)


def setup_inputs(seed: int = 0) -> dict:
    key = jax.random.key(seed)
    ks = jax.random.split(key, 24)
    f32 = jnp.float32
    nrm = lambda k, shape, scale: jax.random.normal(k, shape, f32) * scale
    gain = lambda k, shape: 1.0 + 0.02 * jax.random.normal(k, shape, f32)
    dt = jnp.exp(jax.random.uniform(ks[9], (DEPTH, 2, DN_HEADS), f32,
                                    minval=math.log(1e-3), maxval=math.log(1e-1)))
    return {
        'x_prompt': nrm(ks[0], (BATCH, SEQ, D_MODEL), 1.0),
        'x_sample': nrm(ks[1], (DEC_BATCH, DEC_SEQ, D_MODEL), 1.0),
        'ffn1_norm': gain(ks[2], (DEPTH, D_MODEL)),
        'ffn1_wg': nrm(ks[3], (DEPTH, D_MODEL, D_FF), D_MODEL ** -0.5),
        'ffn1_wu': nrm(ks[4], (DEPTH, D_MODEL, D_FF), D_MODEL ** -0.5),
        'ffn1_wd': nrm(ks[5], (DEPTH, D_FF, D_MODEL), D_FF ** -0.5),
        'mix_norm': gain(ks[6], (DEPTH, D_MODEL)),
        'w_in': nrm(ks[7], (DEPTH, D_MODEL, N_IN), D_MODEL ** -0.5),
        'conv_w': nrm(ks[8], (DEPTH, CONV_K, CONV_CH), CONV_K ** -0.5),
        'dn_a_log': jnp.log(jax.random.uniform(ks[10], (DEPTH, 2, DN_HEADS), f32, minval=1.0, maxval=16.0)),
        'dn_dt_bias': dt + jnp.log(-jnp.expm1(-dt)),
        'dn_out_norm': gain(ks[11], (DEPTH, DN_DV)),
        'diff_lambda': nrm(ks[12], (DEPTH, 4, DA_DQK), 0.1),
        'diff_subln': gain(ks[13], (DEPTH, DA_DV)),
        'w_branch_dn': nrm(ks[14], (DEPTH, DN_WIDTH, D_MODEL), DN_WIDTH ** -0.5),
        'w_branch_da': nrm(ks[15], (DEPTH, DA_WIDTH, D_MODEL), DA_WIDTH ** -0.5),
        'w_out': nrm(ks[16], (DEPTH, D_MODEL, D_MODEL), D_MODEL ** -0.5),
        'ffn2_norm': gain(ks[17], (DEPTH, D_MODEL)),
        'ffn2_wg': nrm(ks[18], (DEPTH, D_MODEL, D_FF), D_MODEL ** -0.5),
        'ffn2_wu': nrm(ks[19], (DEPTH, D_MODEL, D_FF), D_MODEL ** -0.5),
        'ffn2_wd': nrm(ks[20], (DEPTH, D_FF, D_MODEL), D_FF ** -0.5),
        'final_norm': gain(ks[21], (D_MODEL,)),
    }


def reference(x_prompt, x_sample, ffn1_norm, ffn1_wg, ffn1_wu, ffn1_wd, mix_norm, w_in, conv_w,
              dn_a_log, dn_dt_bias, dn_out_norm, diff_lambda, diff_subln, w_branch_dn,
              w_branch_da, w_out, ffn2_norm, ffn2_wg, ffn2_wu, ffn2_wd, final_norm):
    weights = (ffn1_norm, ffn1_wg, ffn1_wu, ffn1_wd, mix_norm, w_in, conv_w, dn_a_log,
               dn_dt_bias, dn_out_norm, diff_lambda, diff_subln, w_branch_dn, w_branch_da,
               w_out, ffn2_norm, ffn2_wg, ffn2_wu, ffn2_wd, final_norm)
    y_prompt = trunk(x_prompt, *weights)
    y_sample = trunk(x_sample, *weights)
    return (y_prompt, y_sample)
```

```python
import functools
import math

import jax
import jax.numpy as jnp
from jax import lax
from jax.experimental import pallas as pl
from jax.experimental.pallas import tpu as pltpu

F32 = jnp.float32
BF16 = jnp.bfloat16

D_MODEL = 1024
SEQ = 2048
DEPTH = 4
D_FF = 2816
HEADS = 4
HEAD_W = 128
DA_DQK = 64
MIX_W = HEADS * HEAD_W
N_GATE = 2 * D_MODEL
NORM_EPS = 1e-6
SUBLN_EPS = 1e-5
CONV_K = 5

LANES = 128
MXU_N = 256
CHUNK = 128
N_CHUNK = SEQ // CHUNK
N_LEVELS = 7
Q_TILE = 256
N_QT = SEQ // Q_TILE
TOK_TILE = 512
FF_TILE = MXU_N
N_FF = D_FF // FF_TILE
VMEM_LIMIT = 56 * 1024 * 1024


def _rms(x, w, eps):
    return x * lax.rsqrt(jnp.mean(x * x, axis=-1, keepdims=True) + eps) * w


def _silu(x):
    return x * jax.nn.sigmoid(x)


def _dot(a, b):
    return jnp.dot(a, b, preferred_element_type=F32)


def _dot_nt(a, b):
    return lax.dot_general(a, b, (((1,), (1,)), ((), ())), preferred_element_type=F32)


def _const_spec(shape):
    zeros = (0,) * len(shape)
    return pl.BlockSpec(shape, lambda *_: zeros, pipeline_mode=pl.Buffered(1))


def _params(n_grid):
    return pltpu.CompilerParams(dimension_semantics=("arbitrary",) * n_grid, vmem_limit_bytes=VMEM_LIMIT)


def _ffn_kernel(*refs, final):
    if final:
        x_ref, nw_ref, wg_ref, wu_ref, wd_ref, fw_ref, o_ref, h_scr, a_scr = refs
    else:
        x_ref, nw_ref, wg_ref, wu_ref, wd_ref, o_ref, h_scr, a_scr = refs
    h_scr[...] = _rms(x_ref[...], nw_ref[...], NORM_EPS).astype(BF16)
    for j in range(N_FF):
        h = h_scr[...]
        g = _dot(h, wg_ref[j])
        u = _dot(h, wu_ref[j])
        a_scr[:, j * FF_TILE:(j + 1) * FF_TILE] = (_silu(g) * u).astype(BF16)
    out = x_ref[...] + 0.5 * _dot(a_scr[...], wd_ref[...])
    if final:
        out = _rms(out, fw_ref[...], NORM_EPS)
    o_ref[...] = out


def _ffn(x, nw, wg, wu, wd, final_w=None):
    m = x.shape[0]
    tok = pl.BlockSpec((TOK_TILE, D_MODEL), lambda i: (i, 0))
    in_specs = [tok, _const_spec((1, D_MODEL)), _const_spec((N_FF, D_MODEL, FF_TILE)),
                _const_spec((N_FF, D_MODEL, FF_TILE)), _const_spec((D_FF, D_MODEL))]
    args = [x, nw, wg, wu, wd]
    if final_w is not None:
        in_specs.append(_const_spec((1, D_MODEL)))
        args.append(final_w)
    return pl.pallas_call(
        functools.partial(_ffn_kernel, final=final_w is not None),
        grid=(m // TOK_TILE,),
        in_specs=in_specs,
        out_specs=tok,
        out_shape=jax.ShapeDtypeStruct((m, D_MODEL), F32),
        scratch_shapes=[pltpu.VMEM((TOK_TILE, D_MODEL), BF16), pltpu.VMEM((TOK_TILE, D_FF), BF16)],
        compiler_params=_params(1),
        name="ffn_final" if final_w is not None else "ffn",
    )(*args)


def _inproj_kernel(x_ref, nw_ref, wdn_ref, wda_ref, wsm_ref, alog_ref, dtb_ref, dn_ref, da_ref, gb_ref):
    h = _rms(x_ref[...], nw_ref[...], NORM_EPS).astype(BF16)
    dn_ref[...] = _dot(h, wdn_ref[...])
    da = _dot(h, wda_ref[...])
    da_ref[:, :MIX_W] = (da[:, :MIX_W] * (DA_DQK ** -0.5)).astype(BF16)
    da_ref[:, MIX_W:] = da[:, MIX_W:].astype(BF16)
    raw = _dot(h, wsm_ref[...])
    lane = lax.broadcasted_iota(jnp.int32, raw.shape, 1)
    t = raw + dtb_ref[...]
    softplus = jnp.maximum(t, 0.0) + jnp.log1p(jnp.exp(-jnp.abs(t)))
    g = -jnp.exp(alog_ref[...]) * softplus
    gb_ref[...] = jnp.where(lane < 2 * HEADS, jax.nn.sigmoid(raw), jnp.where(lane < 4 * HEADS, g, 0.0))


def _inproj(x, nw, wdn, wda, wsm, alog, dtb):
    m = x.shape[0]
    tok = lambda w: pl.BlockSpec((TOK_TILE, w), lambda i: (i, 0))
    return pl.pallas_call(
        _inproj_kernel,
        grid=(m // TOK_TILE,),
        in_specs=[tok(D_MODEL), _const_spec((1, D_MODEL)), _const_spec((D_MODEL, 4 * MIX_W)),
                  _const_spec((D_MODEL, 3 * MIX_W)), _const_spec((D_MODEL, LANES)),
                  _const_spec((1, LANES)), _const_spec((1, LANES))],
        out_specs=[tok(4 * MIX_W), tok(3 * MIX_W), tok(LANES)],
        out_shape=[jax.ShapeDtypeStruct((m, 4 * MIX_W), F32), jax.ShapeDtypeStruct((m, 3 * MIX_W), BF16),
                   jax.ShapeDtypeStruct((m, LANES), F32)],
        compiler_params=_params(1),
        name="inproj",
    )(x, nw, wdn, wda, wsm, alog, dtb)


def _dn_kernel(q_ref, k_ref, v_ref, z_ref, cwq_ref, cwk_ref, cwv_ref, gb_ref, nw_ref, o_ref,
               qb_s, kb_s, kt_s, kf_s, vf_s, gcol_s, bcol_s, u_s, w_s, a_s, gc_s, of_s, ob_s):
    head = pl.program_id(1)
    row = lax.broadcasted_iota(jnp.int32, (SEQ, HEAD_W), 0)

    def conv_silu(x, cw):
        y = cw[2:3, :] * x
        y = y + cw[1:2, :] * jnp.where(row >= 1, pltpu.roll(x, 1, 0), 0.0)
        y = y + cw[0:1, :] * jnp.where(row >= 2, pltpu.roll(x, 2, 0), 0.0)
        y = y + cw[3:4, :] * jnp.where(row <= SEQ - 2, pltpu.roll(x, SEQ - 1, 0), 0.0)
        y = y + cw[4:5, :] * jnp.where(row <= SEQ - 3, pltpu.roll(x, SEQ - 2, 0), 0.0)
        return _silu(y)

    def l2n(x):
        return x * lax.rsqrt(jnp.sum(x * x, axis=-1, keepdims=True) + 1e-6)

    qn = l2n(conv_silu(q_ref[0], cwq_ref[...])) * (HEAD_W ** -0.5)
    kn = l2n(conv_silu(k_ref[0], cwk_ref[...]))
    qb_s[...] = qn.astype(BF16)
    kb_s[...] = kn.astype(BF16)
    kf_s[...] = kn
    vf_s[...] = conv_silu(v_ref[0], cwv_ref[...])
    for n in range(N_CHUNK):
        kt_s[n] = kf_s[n * CHUNK:(n + 1) * CHUNK, :].T.astype(BF16)

    gb = gb_ref[0]
    lane = lax.broadcasted_iota(jnp.int32, (SEQ, LANES), 1)
    for d in range(2):
        beta = jnp.sum(jnp.where(lane == d * HEADS + head, gb, 0.0), axis=-1, keepdims=True)
        g = jnp.sum(jnp.where(lane == 2 * HEADS + d * HEADS + head, gb, 0.0), axis=-1, keepdims=True)
        bcol_s[d] = jnp.broadcast_to(beta, (SEQ, LANES))
        gcol_s[d] = jnp.broadcast_to(g, (SEQ, LANES))

    ri = lax.broadcasted_iota(jnp.int32, (CHUNK, CHUNK), 0)
    ci = lax.broadcasted_iota(jnp.int32, (CHUNK, CHUNK), 1)
    eye = (ri == ci).astype(F32)
    incl = (ri >= ci, ri <= ci)
    strict = (ri > ci, ri < ci)
    last_row = (CHUNK - 1, 0)
    same = lambda sh: (ri >> sh) == (ci >> sh)
    levels = [same(sh + 1) & ~same(sh) for sh in range(N_LEVELS)]

    def total(gc, d):
        return jnp.broadcast_to(gc[last_row[d]:last_row[d] + 1, :], (CHUNK, CHUNK))

    def prep(n, carry):
        r = pl.ds(pl.multiple_of(n * CHUNK, CHUNK), CHUNK)
        kc = kb_s[r, :]
        ktc = kt_s[n]
        kk = _dot(kc, ktc)
        qk = _dot(qb_s[r, :], ktc)
        for d in range(2):
            g = gcol_s[d, r, :]
            hi = g.astype(BF16)
            rem = g - hi.astype(F32)
            mid = rem.astype(BF16)
            lo = (rem - mid.astype(F32)).astype(BF16)
            tri = incl[d].astype(BF16)
            gc = _dot(tri, hi) + _dot(tri, mid) + _dot(tri, lo)
            decay = jnp.exp(jnp.where(incl[d], gc - gc.T, -1e30))
            beta = bcol_s[d, r, :]
            low = jnp.where(strict[d], kk * beta * decay, 0.0)
            t = eye - jnp.where(levels[0], low, 0.0)
            for lvl in levels[1:]:
                off = jnp.where(lvl, low, 0.0).astype(BF16)
                tb = t.astype(BF16)
                t = t - _dot(tb, _dot(off, tb).astype(BF16))
            tb = t.astype(BF16)
            u_s[d, r, :] = _dot(tb, (vf_s[r, :] * beta).astype(BF16))
            w_s[d, r, :] = _dot(tb, (kf_s[r, :] * (beta * jnp.exp(gc))).astype(BF16)).astype(BF16)
            a_s[d, n] = (qk * decay).astype(BF16)
            gc_s[d, r, :] = gc
        return carry

    lax.fori_loop(0, N_CHUNK, prep, 0)

    def scan(t, states):
        new = []
        for d in range(2):
            n = t if d == 0 else N_CHUNK - 1 - t
            r = pl.ds(pl.multiple_of(n * CHUNK, CHUNK), CHUNK)
            sb = states[d].astype(BF16)
            gc = gc_s[d, r, :]
            gt = total(gc, d)
            delta = u_s[d, r, :] - _dot(w_s[d, r, :], sb)
            o = _dot(qb_s[r, :], sb) * jnp.exp(gc) + _dot(a_s[d, n], delta.astype(BF16))
            (of_s, ob_s)[d][r, :] = o
            new.append(states[d] * jnp.exp(gt) + _dot(kt_s[n], (delta * jnp.exp(gt - gc)).astype(BF16)))
        return tuple(new)

    zero = jnp.zeros((HEAD_W, HEAD_W), F32)
    lax.fori_loop(0, N_CHUNK, scan, (zero, zero))

    o = _rms(of_s[...] + ob_s[...], nw_ref[...], NORM_EPS) * _silu(z_ref[0])
    o_ref[0] = o.astype(BF16)


def _deltanet(dn, conv_w, gb, norm_w):
    b = dn.shape[0]
    blk = lambda off: pl.BlockSpec((1, SEQ, HEAD_W), lambda i, h: (i, 0, off + h))
    cw = lambda off: pl.BlockSpec((CONV_K, HEAD_W), lambda i, h: (0, off + h))
    seq_f32 = pltpu.VMEM((SEQ, HEAD_W), F32)
    seq_bf16 = pltpu.VMEM((SEQ, HEAD_W), BF16)
    chunks_bf16 = pltpu.VMEM((N_CHUNK, CHUNK, CHUNK), BF16)
    return pl.pallas_call(
        _dn_kernel,
        grid=(b, HEADS),
        in_specs=[blk(0), blk(HEADS), blk(2 * HEADS), blk(3 * HEADS), cw(0), cw(HEADS), cw(2 * HEADS),
                  pl.BlockSpec((1, SEQ, LANES), lambda i, h: (i, 0, 0)), _const_spec((1, HEAD_W))],
        out_specs=pl.BlockSpec((1, SEQ, HEAD_W), lambda i, h: (i, 0, h)),
        out_shape=jax.ShapeDtypeStruct((b, SEQ, MIX_W), BF16),
        scratch_shapes=[seq_bf16, seq_bf16, chunks_bf16, seq_f32, seq_f32,
                        pltpu.VMEM((2, SEQ, LANES), F32), pltpu.VMEM((2, SEQ, LANES), F32),
                        pltpu.VMEM((2, SEQ, HEAD_W), F32), pltpu.VMEM((2, SEQ, HEAD_W), BF16),
                        pltpu.VMEM((2, N_CHUNK, CHUNK, CHUNK), BF16), pltpu.VMEM((2, SEQ, LANES), F32),
                        seq_f32, seq_f32],
        compiler_params=_params(2),
        name="deltanet",
    )(dn, dn, dn, dn, conv_w, conv_w, conv_w, gb, norm_w)


def _da_kernel(q_ref, k_ref, v_ref, tab_ref, lp_ref, nw_ref, o_ref, *, lambda_init):
    lp = lp_ref[...]
    lam = (jnp.exp(jnp.sum(lp[0:1, :] * lp[1:2, :], axis=-1, keepdims=True))
           - jnp.exp(jnp.sum(lp[2:3, :] * lp[3:4, :], axis=-1, keepdims=True)) + lambda_init)
    k = k_ref[0]
    v = v_ref[0]
    lane = lax.broadcasted_iota(jnp.int32, (Q_TILE, HEAD_W), 1)
    for qt in range(N_QT):
        q = q_ref[0, qt * Q_TILE:(qt + 1) * Q_TILE, :]
        start = (N_QT - 1 - qt) * Q_TILE
        bias = tab_ref[0, :, start:start + SEQ]
        outs = []
        for m in range(2):
            qm = jnp.where((lane >= m * DA_DQK) & (lane < (m + 1) * DA_DQK), q, jnp.zeros_like(q))
            s = _dot_nt(qm, k) + bias
            p = jnp.exp(s - jnp.max(s, axis=-1, keepdims=True))
            outs.append(_dot(p.astype(BF16), v) / jnp.sum(p, axis=-1, keepdims=True))
        o = outs[0] - lam * outs[1]
        o = _rms(o, nw_ref[...], SUBLN_EPS) * (1.0 - lambda_init)
        o_ref[0, qt * Q_TILE:(qt + 1) * Q_TILE, :] = o.astype(BF16)


def _diff_attention(da, table, lam_params, subln_w, lambda_init):
    b = da.shape[0]
    blk = lambda off: pl.BlockSpec((1, SEQ, HEAD_W), lambda h, i: (i, 0, off + h))
    tab_w = (2 * N_QT - 1) * Q_TILE
    return pl.pallas_call(
        functools.partial(_da_kernel, lambda_init=lambda_init),
        grid=(HEADS, b),
        in_specs=[blk(0), blk(HEADS), blk(2 * HEADS),
                  pl.BlockSpec((1, Q_TILE, tab_w), lambda h, i: (h, 0, 0)),
                  _const_spec((4, DA_DQK)), _const_spec((1, HEAD_W))],
        out_specs=pl.BlockSpec((1, SEQ, HEAD_W), lambda h, i: (i, 0, h)),
        out_shape=jax.ShapeDtypeStruct((b, SEQ, MIX_W), BF16),
        compiler_params=_params(2),
        name="diff_attention",
    )(da, da, da, table, lam_params, subln_w)


def _alibi_table():
    slopes = 2.0 ** (-8.0 * jnp.arange(1, HEADS + 1, dtype=F32) / HEADS)
    i = jnp.arange(Q_TILE, dtype=jnp.int32)[:, None]
    c = jnp.arange((2 * N_QT - 1) * Q_TILE, dtype=jnp.int32)[None, :] - (N_QT - 1) * Q_TILE
    return -slopes[:, None, None] * jnp.abs(i - c).astype(F32)[None]


def _merge_kernel(x_ref, dn_ref, da_ref, nw_ref, wgate_ref, wbdn_ref, wbda_ref, wout_ref, o_ref):
    x = x_ref[...]
    h = _rms(x, nw_ref[...], NORM_EPS).astype(BF16)
    gates = jax.nn.sigmoid(_dot(h, wgate_ref[...]))
    y_dn = _dot(dn_ref[...], wbdn_ref[...])
    y_da = _dot(da_ref[...], wbda_ref[...])
    merged = gates[:, :D_MODEL] * y_dn + gates[:, D_MODEL:] * y_da
    o_ref[...] = x + _dot(merged.astype(BF16), wout_ref[...])


def _merge(x, o_dn, o_da, nw, wgate, wbdn, wbda, wout):
    m = x.shape[0]
    tok = lambda w: pl.BlockSpec((TOK_TILE, w), lambda i: (i, 0))
    return pl.pallas_call(
        _merge_kernel,
        grid=(m // TOK_TILE,),
        in_specs=[tok(D_MODEL), tok(MIX_W), tok(MIX_W), _const_spec((1, D_MODEL)),
                  _const_spec((D_MODEL, N_GATE)), _const_spec((MIX_W, D_MODEL)),
                  _const_spec((MIX_W, D_MODEL)), _const_spec((D_MODEL, D_MODEL))],
        out_specs=tok(D_MODEL),
        out_shape=jax.ShapeDtypeStruct((m, D_MODEL), F32),
        compiler_params=_params(1),
        name="merge",
    )(x, o_dn, o_da, nw, wgate, wbdn, wbda, wout)


def _ff_tiles(w):
    return w.reshape(D_MODEL, N_FF, FF_TILE).transpose(1, 0, 2).astype(BF16)


def _lane_row(v):
    return jnp.zeros((1, LANES), F32).at[0, 2 * HEADS:4 * HEADS].set(v.reshape(-1).astype(F32))


def kernel(x_prompt, x_sample, ffn1_norm, ffn1_wg, ffn1_wu, ffn1_wd, mix_norm, w_in, conv_w, dn_a_log, dn_dt_bias, dn_out_norm, diff_lambda, diff_subln, w_branch_dn, w_branch_da, w_out, ffn2_norm, ffn2_wg, ffn2_wu, ffn2_wd, final_norm):
    n_prompt = x_prompt.shape[0]
    batch = n_prompt + x_sample.shape[0]
    x = jnp.concatenate([x_prompt, x_sample], axis=0).reshape(batch * SEQ, D_MODEL)
    table = _alibi_table()
    row = lambda v: v.reshape(1, -1).astype(F32)
    c_dn, c_sm, c_da, c_gate = 4 * MIX_W, 4 * MIX_W + 4 * HEADS, 4 * MIX_W + 4 * HEADS + 3 * MIX_W, w_in.shape[-1]
    for l in range(DEPTH):
        x = _ffn(x, row(ffn1_norm[l]), _ff_tiles(ffn1_wg[l]), _ff_tiles(ffn1_wu[l]), ffn1_wd[l].astype(BF16))
        wl = w_in[l]
        wsm = jnp.zeros((D_MODEL, LANES), BF16).at[:, :4 * HEADS].set(wl[:, c_dn:c_sm].astype(BF16))
        dn, da, gb = _inproj(x, row(mix_norm[l]), wl[:, :c_dn].astype(BF16), wl[:, c_sm:c_da].astype(BF16), wsm,
                             _lane_row(dn_a_log[l]), _lane_row(dn_dt_bias[l]))
        o_dn = _deltanet(dn.reshape(batch, SEQ, 4 * MIX_W), conv_w[l].astype(F32),
                         gb.reshape(batch, SEQ, LANES), row(dn_out_norm[l]))
        lambda_init = 0.8 - 0.6 * math.exp(-0.3 * l)
        o_da = _diff_attention(da.reshape(batch, SEQ, 3 * MIX_W), table, diff_lambda[l].astype(F32),
                               row(diff_subln[l]), lambda_init)
        x = _merge(x, o_dn.reshape(batch * SEQ, MIX_W), o_da.reshape(batch * SEQ, MIX_W), row(mix_norm[l]),
                   wl[:, c_da:c_gate].astype(BF16), w_branch_dn[l].astype(BF16), w_branch_da[l].astype(BF16),
                   w_out[l].astype(BF16))
        x = _ffn(x, row(ffn2_norm[l]), _ff_tiles(ffn2_wg[l]), _ff_tiles(ffn2_wu[l]), ffn2_wd[l].astype(BF16),
                 final_w=row(final_norm) if l == DEPTH - 1 else None)
    y = x.reshape(batch, SEQ, D_MODEL)
    return (y[:n_prompt], y[n_prompt:])
```

```python
import functools
import math

import jax
import jax.numpy as jnp
from jax import lax
from jax.experimental import pallas as pl
from jax.experimental.pallas import tpu as pltpu

F32 = jnp.float32
BF16 = jnp.bfloat16

D_MODEL = 1024
SEQ = 2048
DEPTH = 4
D_FF = 2816
HEADS = 4
HEAD_W = 128
DA_DQK = 64
MIX_W = HEADS * HEAD_W
N_GATE = 2 * D_MODEL
NORM_EPS = 1e-6
SUBLN_EPS = 1e-5
CONV_K = 5
CONV_PAD = 8

LANES = 128
MXU_N = 256
CHUNK = 128
N_CHUNK = SEQ // CHUNK
N_LEVELS = 7
PREP_CHUNKS = 16
Q_TILE = 256
N_QT = SEQ // Q_TILE
TOK_TILE = 512
FF_TILE = MXU_N
N_FF = D_FF // FF_TILE
VMEM_LIMIT = 56 * 1024 * 1024


def _rms(x, w, eps):
    return x * lax.rsqrt(jnp.mean(x * x, axis=-1, keepdims=True) + eps) * w


def _silu(x):
    return x * jax.nn.sigmoid(x)


def _dot(a, b):
    return jnp.dot(a, b, preferred_element_type=F32)


def _dot_nt(a, b):
    return lax.dot_general(a, b, (((1,), (1,)), ((), ())), preferred_element_type=F32)


def _const_spec(shape):
    zeros = (0,) * len(shape)
    return pl.BlockSpec(shape, lambda *_: zeros, pipeline_mode=pl.Buffered(1))


def _params(n_grid):
    return pltpu.CompilerParams(dimension_semantics=("arbitrary",) * n_grid, vmem_limit_bytes=VMEM_LIMIT)


def _ffn_kernel(*refs, final):
    if final:
        x_ref, nw_ref, wg_ref, wu_ref, wd_ref, fw_ref, o_ref, h_scr, a_scr = refs
    else:
        x_ref, nw_ref, wg_ref, wu_ref, wd_ref, o_ref, h_scr, a_scr = refs
    h_scr[...] = _rms(x_ref[...], nw_ref[...], NORM_EPS).astype(BF16)
    for j in range(N_FF):
        h = h_scr[...]
        g = _dot(h, wg_ref[j])
        u = _dot(h, wu_ref[j])
        a_scr[:, j * FF_TILE:(j + 1) * FF_TILE] = (_silu(g) * u).astype(BF16)
    out = x_ref[...] + 0.5 * _dot(a_scr[...], wd_ref[...])
    if final:
        out = _rms(out, fw_ref[...], NORM_EPS)
    o_ref[...] = out


def _ffn(x, nw, wg, wu, wd, final_w=None):
    m = x.shape[0]
    tok = pl.BlockSpec((TOK_TILE, D_MODEL), lambda i: (i, 0))
    in_specs = [tok, _const_spec((1, D_MODEL)), _const_spec((N_FF, D_MODEL, FF_TILE)),
                _const_spec((N_FF, D_MODEL, FF_TILE)), _const_spec((D_FF, D_MODEL))]
    args = [x, nw, wg, wu, wd]
    if final_w is not None:
        in_specs.append(_const_spec((1, D_MODEL)))
        args.append(final_w)
    return pl.pallas_call(
        functools.partial(_ffn_kernel, final=final_w is not None),
        grid=(m // TOK_TILE,),
        in_specs=in_specs,
        out_specs=tok,
        out_shape=jax.ShapeDtypeStruct((m, D_MODEL), F32),
        scratch_shapes=[pltpu.VMEM((TOK_TILE, D_MODEL), BF16), pltpu.VMEM((TOK_TILE, D_FF), BF16)],
        compiler_params=_params(1),
        name="ffn_final" if final_w is not None else "ffn",
    )(*args)


def _inproj_kernel(x_ref, nw_ref, wdn_ref, wda_ref, wsm_ref, alog_ref, dtb_ref, dn_ref, da_ref, gb_ref):
    h = _rms(x_ref[...], nw_ref[...], NORM_EPS).astype(BF16)
    dn_ref[...] = _dot(h, wdn_ref[...])
    da = _dot(h, wda_ref[...])
    da_ref[:, :MIX_W] = (da[:, :MIX_W] * (DA_DQK ** -0.5)).astype(BF16)
    da_ref[:, MIX_W:] = da[:, MIX_W:].astype(BF16)
    raw = _dot(h, wsm_ref[...])
    lane = lax.broadcasted_iota(jnp.int32, raw.shape, 1)
    t = raw + dtb_ref[...]
    softplus = jnp.maximum(t, 0.0) + jnp.log1p(jnp.exp(-jnp.abs(t)))
    g = -jnp.exp(alog_ref[...]) * softplus
    gb_ref[...] = jnp.where(lane < 2 * HEADS, jax.nn.sigmoid(raw), jnp.where(lane < 4 * HEADS, g, 0.0))


def _inproj(x, nw, wdn, wda, wsm, alog, dtb):
    m = x.shape[0]
    tok = lambda w: pl.BlockSpec((TOK_TILE, w), lambda i: (i, 0))
    return pl.pallas_call(
        _inproj_kernel,
        grid=(m // TOK_TILE,),
        in_specs=[tok(D_MODEL), _const_spec((1, D_MODEL)), _const_spec((D_MODEL, 4 * MIX_W)),
                  _const_spec((D_MODEL, 3 * MIX_W)), _const_spec((D_MODEL, LANES)),
                  _const_spec((1, LANES)), _const_spec((1, LANES))],
        out_specs=[tok(4 * MIX_W), tok(3 * MIX_W), tok(LANES)],
        out_shape=[jax.ShapeDtypeStruct((m, 4 * MIX_W), F32), jax.ShapeDtypeStruct((m, 3 * MIX_W), BF16),
                   jax.ShapeDtypeStruct((m, LANES), F32)],
        compiler_params=_params(1),
        name="inproj",
    )(x, nw, wdn, wda, wsm, alog, dtb)


def _dn_kernel(q_ref, k_ref, v_ref, z_ref, cwq_ref, cwk_ref, cwv_ref, gb_ref, nw_ref, o_ref,
               pad_s, qb_s, kb_s, kt_s, kf_s, vf_s, gcol_s, bcol_s, psi_s, m_s, egt_s, of_s, ob_s):
    head = pl.program_id(1)
    pad_s[:CONV_PAD, :] = jnp.zeros((CONV_PAD, HEAD_W), F32)
    pad_s[CONV_PAD + SEQ:, :] = jnp.zeros((CONV_PAD, HEAD_W), F32)

    def conv_silu(x_ref, cw_ref):
        pad_s[CONV_PAD:CONV_PAD + SEQ, :] = x_ref[0]
        first = CONV_PAD - CONV_K // 2
        y = cw_ref[0:1, :] * pad_s[first:first + SEQ, :]
        for j in range(1, CONV_K):
            y = y + cw_ref[j:j + 1, :] * pad_s[first + j:first + j + SEQ, :]
        return _silu(y)

    def l2n(x):
        return x * lax.rsqrt(jnp.sum(x * x, axis=-1, keepdims=True) + 1e-6)

    qn = l2n(conv_silu(q_ref, cwq_ref)) * (HEAD_W ** -0.5)
    kn = l2n(conv_silu(k_ref, cwk_ref))
    qb_s[...] = qn.astype(BF16)
    kb_s[...] = kn.astype(BF16)
    kf_s[...] = kn
    vf_s[...] = conv_silu(v_ref, cwv_ref)
    for n in range(N_CHUNK):
        kt_s[n] = kf_s[n * CHUNK:(n + 1) * CHUNK, :].T.astype(BF16)

    gb = gb_ref[0]
    lane = lax.broadcasted_iota(jnp.int32, (SEQ, LANES), 1)
    for d in range(2):
        beta = jnp.sum(jnp.where(lane == d * HEADS + head, gb, 0.0), axis=-1, keepdims=True)
        g = jnp.sum(jnp.where(lane == 2 * HEADS + d * HEADS + head, gb, 0.0), axis=-1, keepdims=True)
        bcol_s[d] = jnp.broadcast_to(beta, (SEQ, LANES))
        gcol_s[d] = jnp.broadcast_to(g, (SEQ, LANES))

    ri = lax.broadcasted_iota(jnp.int32, (CHUNK, CHUNK), 0)
    ci = lax.broadcasted_iota(jnp.int32, (CHUNK, CHUNK), 1)
    eye = (ri == ci).astype(F32)
    incl = (ri >= ci, ri <= ci)
    strict = (ri > ci, ri < ci)
    last_row = (CHUNK - 1, 0)
    same = lambda sh: (ri >> sh) == (ci >> sh)
    levels = [same(sh + 1) & ~same(sh) for sh in range(N_LEVELS)]

    def total(gc, d):
        return jnp.broadcast_to(gc[last_row[d]:last_row[d] + 1, :], (CHUNK, CHUNK))

    tri = [m.astype(BF16) for m in incl]

    def prep(i, carry):
        ns = [i * PREP_CHUNKS + c for c in range(PREP_CHUNKS)]
        rows = [pl.ds(pl.multiple_of(n * CHUNK, CHUNK), CHUNK) for n in ns]
        chains = [(c, d) for c in range(PREP_CHUNKS) for d in range(2)]
        kk = [_dot(kb_s[r, :], kt_s[n]) for n, r in zip(ns, rows)]
        qk = [_dot(qb_s[r, :], kt_s[n]) for n, r in zip(ns, rows)]
        gc = {}
        for c, d in chains:
            g = gcol_s[d, rows[c], :]
            hi = g.astype(BF16)
            rem = g - hi.astype(F32)
            mid = rem.astype(BF16)
            lo = (rem - mid.astype(F32)).astype(BF16)
            gc[c, d] = _dot(tri[d], hi) + _dot(tri[d], mid) + _dot(tri[d], lo)
        low, t, a = {}, {}, {}
        for c, d in chains:
            decay = jnp.exp(jnp.where(incl[d], gc[c, d] - gc[c, d].T, -1e30))
            a[c, d] = (qk[c] * decay).astype(BF16)
            low[c, d] = jnp.where(strict[d], kk[c] * bcol_s[d, rows[c], :] * decay, 0.0)
            t[c, d] = eye - jnp.where(levels[0], low[c, d], 0.0)
        for lvl in levels[1:]:
            tb = {k: v.astype(BF16) for k, v in t.items()}
            x = {k: _dot(jnp.where(lvl, low[k], 0.0).astype(BF16), tb[k]).astype(BF16) for k in chains}
            t = {k: t[k] - _dot(tb[k], x[k]) for k in chains}
        uw = {}
        for c, d in chains:
            beta = bcol_s[d, rows[c], :]
            rhs = jnp.concatenate([vf_s[rows[c], :] * beta, kf_s[rows[c], :] * (beta * jnp.exp(gc[c, d]))], axis=1)
            uw[c, d] = _dot(t[c, d].astype(BF16), rhs.astype(BF16))
        au, pw = {}, {}
        for c, d in chains:
            gt = total(gc[c, d], d)
            erc = jnp.exp(gt - gc[c, d])
            scaled = jnp.concatenate([uw[c, d][:, :HEAD_W] * erc, uw[c, d][:, HEAD_W:] * erc], axis=1)
            au[c, d] = _dot(a[c, d], uw[c, d].astype(BF16))
            pw[c, d] = _dot(kt_s[ns[c]], scaled.astype(BF16))
            egt_s[d, ns[c]] = jnp.exp(gt[:8, :])
        for c, d in chains:
            (of_s, ob_s)[d][rows[c], :] = au[c, d][:, :HEAD_W]
            psi_s[d, rows[c], :] = pw[c, d][:, :HEAD_W]
            m_s[d, ns[c], :CHUNK, :] = pw[c, d][:, HEAD_W:].astype(BF16)
            qe = qb_s[rows[c], :].astype(F32) * jnp.exp(gc[c, d])
            m_s[d, ns[c], CHUNK:, :] = (qe - au[c, d][:, HEAD_W:]).astype(BF16)
        return carry

    lax.fori_loop(0, N_CHUNK // PREP_CHUNKS, prep, 0)

    def scan(step, states):
        new = []
        for d in range(2):
            n = step if d == 0 else N_CHUNK - 1 - step
            r = pl.ds(pl.multiple_of(n * CHUNK, CHUNK), CHUNK)
            prod = _dot(m_s[d, n], states[d].astype(BF16))
            o_ref_d = (of_s, ob_s)[d]
            o_ref_d[r, :] = o_ref_d[r, :] + prod[CHUNK:, :]
            new.append(states[d] * egt_s[d, n][0:1, :] - prod[:CHUNK, :] + psi_s[d, r, :])
        return tuple(new)

    zero = jnp.zeros((HEAD_W, HEAD_W), F32)
    lax.fori_loop(0, N_CHUNK, scan, (zero, zero))

    o = _rms(of_s[...] + ob_s[...], nw_ref[...], NORM_EPS) * _silu(z_ref[0])
    o_ref[0] = o.astype(BF16)


def _deltanet(dn, conv_w, gb, norm_w):
    b = dn.shape[0]
    blk = lambda off: pl.BlockSpec((1, SEQ, HEAD_W), lambda i, h: (i, 0, off + h))
    cw = lambda off: pl.BlockSpec((CONV_K, HEAD_W), lambda i, h: (0, off + h))
    seq_f32 = pltpu.VMEM((SEQ, HEAD_W), F32)
    seq_bf16 = pltpu.VMEM((SEQ, HEAD_W), BF16)
    chunks_bf16 = pltpu.VMEM((N_CHUNK, CHUNK, CHUNK), BF16)
    return pl.pallas_call(
        _dn_kernel,
        grid=(b, HEADS),
        in_specs=[blk(0), blk(HEADS), blk(2 * HEADS), blk(3 * HEADS), cw(0), cw(HEADS), cw(2 * HEADS),
                  pl.BlockSpec((1, SEQ, LANES), lambda i, h: (i, 0, 0)), _const_spec((1, HEAD_W))],
        out_specs=pl.BlockSpec((1, SEQ, HEAD_W), lambda i, h: (i, 0, h)),
        out_shape=jax.ShapeDtypeStruct((b, SEQ, MIX_W), BF16),
        scratch_shapes=[pltpu.VMEM((SEQ + 2 * CONV_PAD, HEAD_W), F32),
                        seq_bf16, seq_bf16, chunks_bf16, seq_f32, seq_f32,
                        pltpu.VMEM((2, SEQ, LANES), F32), pltpu.VMEM((2, SEQ, LANES), F32),
                        pltpu.VMEM((2, SEQ, HEAD_W), F32), pltpu.VMEM((2, N_CHUNK, 2 * CHUNK, HEAD_W), BF16),
                        pltpu.VMEM((2, N_CHUNK, 8, LANES), F32), seq_f32, seq_f32],
        compiler_params=_params(2),
        name="deltanet",
    )(dn, dn, dn, dn, conv_w, conv_w, conv_w, gb, norm_w)


def _da_kernel(q_ref, k_ref, v_ref, tab_ref, lp_ref, nw_ref, o_ref, *, lambda_init):
    lp = lp_ref[...]
    lam = (jnp.exp(jnp.sum(lp[0:1, :] * lp[1:2, :], axis=-1, keepdims=True))
           - jnp.exp(jnp.sum(lp[2:3, :] * lp[3:4, :], axis=-1, keepdims=True)) + lambda_init)
    k = k_ref[0]
    v = v_ref[0]
    lane = lax.broadcasted_iota(jnp.int32, (Q_TILE, HEAD_W), 1)
    for qt in range(N_QT):
        q = q_ref[0, qt * Q_TILE:(qt + 1) * Q_TILE, :]
        start = (N_QT - 1 - qt) * Q_TILE
        bias = tab_ref[0, :, start:start + SEQ]
        outs = []
        for m in range(2):
            qm = jnp.where((lane >= m * DA_DQK) & (lane < (m + 1) * DA_DQK), q, jnp.zeros_like(q))
            s = _dot_nt(qm, k) + bias
            p = jnp.exp(s - jnp.max(s, axis=-1, keepdims=True))
            outs.append(_dot(p.astype(BF16), v) / jnp.sum(p, axis=-1, keepdims=True))
        o = outs[0] - lam * outs[1]
        o = _rms(o, nw_ref[...], SUBLN_EPS) * (1.0 - lambda_init)
        o_ref[0, qt * Q_TILE:(qt + 1) * Q_TILE, :] = o.astype(BF16)


def _diff_attention(da, table, lam_params, subln_w, lambda_init):
    b = da.shape[0]
    blk = lambda off: pl.BlockSpec((1, SEQ, HEAD_W), lambda h, i: (i, 0, off + h))
    tab_w = (2 * N_QT - 1) * Q_TILE
    return pl.pallas_call(
        functools.partial(_da_kernel, lambda_init=lambda_init),
        grid=(HEADS, b),
        in_specs=[blk(0), blk(HEADS), blk(2 * HEADS),
                  pl.BlockSpec((1, Q_TILE, tab_w), lambda h, i: (h, 0, 0)),
                  _const_spec((4, DA_DQK)), _const_spec((1, HEAD_W))],
        out_specs=pl.BlockSpec((1, SEQ, HEAD_W), lambda h, i: (i, 0, h)),
        out_shape=jax.ShapeDtypeStruct((b, SEQ, MIX_W), BF16),
        compiler_params=_params(2),
        name="diff_attention",
    )(da, da, da, table, lam_params, subln_w)


def _alibi_table():
    slopes = 2.0 ** (-8.0 * jnp.arange(1, HEADS + 1, dtype=F32) / HEADS)
    i = jnp.arange(Q_TILE, dtype=jnp.int32)[:, None]
    c = jnp.arange((2 * N_QT - 1) * Q_TILE, dtype=jnp.int32)[None, :] - (N_QT - 1) * Q_TILE
    return -slopes[:, None, None] * jnp.abs(i - c).astype(F32)[None]


def _merge_kernel(x_ref, dn_ref, da_ref, nw_ref, wgate_ref, wbdn_ref, wbda_ref, wout_ref, o_ref):
    x = x_ref[...]
    h = _rms(x, nw_ref[...], NORM_EPS).astype(BF16)
    gates = jax.nn.sigmoid(_dot(h, wgate_ref[...]))
    y_dn = _dot(dn_ref[...], wbdn_ref[...])
    y_da = _dot(da_ref[...], wbda_ref[...])
    merged = gates[:, :D_MODEL] * y_dn + gates[:, D_MODEL:] * y_da
    o_ref[...] = x + _dot(merged.astype(BF16), wout_ref[...])


def _merge(x, o_dn, o_da, nw, wgate, wbdn, wbda, wout):
    m = x.shape[0]
    tok = lambda w: pl.BlockSpec((TOK_TILE, w), lambda i: (i, 0))
    return pl.pallas_call(
        _merge_kernel,
        grid=(m // TOK_TILE,),
        in_specs=[tok(D_MODEL), tok(MIX_W), tok(MIX_W), _const_spec((1, D_MODEL)),
                  _const_spec((D_MODEL, N_GATE)), _const_spec((MIX_W, D_MODEL)),
                  _const_spec((MIX_W, D_MODEL)), _const_spec((D_MODEL, D_MODEL))],
        out_specs=tok(D_MODEL),
        out_shape=jax.ShapeDtypeStruct((m, D_MODEL), F32),
        compiler_params=_params(1),
        name="merge",
    )(x, o_dn, o_da, nw, wgate, wbdn, wbda, wout)


def _ff_tiles(w):
    return w.reshape(D_MODEL, N_FF, FF_TILE).transpose(1, 0, 2).astype(BF16)


def _lane_row(v):
    return jnp.zeros((1, LANES), F32).at[0, 2 * HEADS:4 * HEADS].set(v.reshape(-1).astype(F32))


def kernel(x_prompt, x_sample, ffn1_norm, ffn1_wg, ffn1_wu, ffn1_wd, mix_norm, w_in, conv_w, dn_a_log, dn_dt_bias, dn_out_norm, diff_lambda, diff_subln, w_branch_dn, w_branch_da, w_out, ffn2_norm, ffn2_wg, ffn2_wu, ffn2_wd, final_norm):
    n_prompt = x_prompt.shape[0]
    batch = n_prompt + x_sample.shape[0]
    x = jnp.concatenate([x_prompt, x_sample], axis=0).reshape(batch * SEQ, D_MODEL)
    table = _alibi_table()
    row = lambda v: v.reshape(1, -1).astype(F32)
    c_dn, c_sm, c_da, c_gate = 4 * MIX_W, 4 * MIX_W + 4 * HEADS, 4 * MIX_W + 4 * HEADS + 3 * MIX_W, w_in.shape[-1]
    for l in range(DEPTH):
        x = _ffn(x, row(ffn1_norm[l]), _ff_tiles(ffn1_wg[l]), _ff_tiles(ffn1_wu[l]), ffn1_wd[l].astype(BF16))
        wl = w_in[l]
        wsm = jnp.zeros((D_MODEL, LANES), BF16).at[:, :4 * HEADS].set(wl[:, c_dn:c_sm].astype(BF16))
        dn, da, gb = _inproj(x, row(mix_norm[l]), wl[:, :c_dn].astype(BF16), wl[:, c_sm:c_da].astype(BF16), wsm,
                             _lane_row(dn_a_log[l]), _lane_row(dn_dt_bias[l]))
        o_dn = _deltanet(dn.reshape(batch, SEQ, 4 * MIX_W), conv_w[l].astype(F32),
                         gb.reshape(batch, SEQ, LANES), row(dn_out_norm[l]))
        lambda_init = 0.8 - 0.6 * math.exp(-0.3 * l)
        o_da = _diff_attention(da.reshape(batch, SEQ, 3 * MIX_W), table, diff_lambda[l].astype(F32),
                               row(diff_subln[l]), lambda_init)
        x = _merge(x, o_dn.reshape(batch * SEQ, MIX_W), o_da.reshape(batch * SEQ, MIX_W), row(mix_norm[l]),
                   wl[:, c_da:c_gate].astype(BF16), w_branch_dn[l].astype(BF16), w_branch_da[l].astype(BF16),
                   w_out[l].astype(BF16))
        x = _ffn(x, row(ffn2_norm[l]), _ff_tiles(ffn2_wg[l]), _ff_tiles(ffn2_wu[l]), ffn2_wd[l].astype(BF16),
                 final_w=row(final_norm) if l == DEPTH - 1 else None)
    y = x.reshape(batch, SEQ, D_MODEL)
    return (y[:n_prompt], y[n_prompt:])
```

```python
import functools
import math

import jax
import jax.numpy as jnp
from jax import lax
from jax.experimental import pallas as pl
from jax.experimental.pallas import tpu as pltpu

F32 = jnp.float32
BF16 = jnp.bfloat16

D_MODEL = 1024
SEQ = 2048
DEPTH = 4
D_FF = 2816
HEADS = 4
HEAD_W = 128
DA_DQK = 64
MIX_W = HEADS * HEAD_W
N_GATE = 2 * D_MODEL
NORM_EPS = 1e-6
SUBLN_EPS = 1e-5
CONV_K = 5
CONV_PAD = 8

LOG2_E = math.log2(math.e)
LANES = 128
MXU_N = 256
CHUNK = 128
N_CHUNK = SEQ // CHUNK
N_LEVELS = 7
PREP_CHUNKS = 16
Q_TILE = 256
N_QT = SEQ // Q_TILE
SCORE_LOOKAHEAD = 1
TOK_TILE = 512
FF_TILE = MXU_N
N_FF = D_FF // FF_TILE
VMEM_LIMIT = 56 * 1024 * 1024


def _rms(x, w, eps):
    return x * lax.rsqrt(jnp.mean(x * x, axis=-1, keepdims=True) + eps) * w


def _silu(x):
    return x * jax.nn.sigmoid(x)


def _dot(a, b):
    return jnp.dot(a, b, preferred_element_type=F32)


def _dot_nt(a, b):
    return lax.dot_general(a, b, (((1,), (1,)), ((), ())), preferred_element_type=F32)


def _const_spec(shape):
    zeros = (0,) * len(shape)
    return pl.BlockSpec(shape, lambda *_: zeros, pipeline_mode=pl.Buffered(1))


def _params(n_grid):
    return pltpu.CompilerParams(dimension_semantics=("arbitrary",) * n_grid, vmem_limit_bytes=VMEM_LIMIT)


def _ffn_kernel(*refs, final):
    if final:
        x_ref, nw_ref, wg_ref, wu_ref, wd_ref, fw_ref, o_ref, h_scr, a_scr = refs
    else:
        x_ref, nw_ref, wg_ref, wu_ref, wd_ref, o_ref, h_scr, a_scr = refs
    h_scr[...] = _rms(x_ref[...], nw_ref[...], NORM_EPS).astype(BF16)
    for j in range(N_FF):
        h = h_scr[...]
        g = _dot(h, wg_ref[j])
        u = _dot(h, wu_ref[j])
        a_scr[:, j * FF_TILE:(j + 1) * FF_TILE] = (_silu(g) * u).astype(BF16)
    out = x_ref[...] + 0.5 * _dot(a_scr[...], wd_ref[...])
    if final:
        out = _rms(out, fw_ref[...], NORM_EPS)
    o_ref[...] = out


def _ffn(x, nw, wg, wu, wd, final_w=None):
    m = x.shape[0]
    tok = pl.BlockSpec((TOK_TILE, D_MODEL), lambda i: (i, 0))
    in_specs = [tok, _const_spec((1, D_MODEL)), _const_spec((N_FF, D_MODEL, FF_TILE)),
                _const_spec((N_FF, D_MODEL, FF_TILE)), _const_spec((D_FF, D_MODEL))]
    args = [x, nw, wg, wu, wd]
    if final_w is not None:
        in_specs.append(_const_spec((1, D_MODEL)))
        args.append(final_w)
    return pl.pallas_call(
        functools.partial(_ffn_kernel, final=final_w is not None),
        grid=(m // TOK_TILE,),
        in_specs=in_specs,
        out_specs=tok,
        out_shape=jax.ShapeDtypeStruct((m, D_MODEL), F32),
        scratch_shapes=[pltpu.VMEM((TOK_TILE, D_MODEL), BF16), pltpu.VMEM((TOK_TILE, D_FF), BF16)],
        compiler_params=_params(1),
        name="ffn_final" if final_w is not None else "ffn",
    )(*args)


def _inproj_kernel(x_ref, nw_ref, wdn_ref, wda_ref, wsm_ref, alog_ref, dtb_ref, dn_ref, da_ref, gb_ref):
    h = _rms(x_ref[...], nw_ref[...], NORM_EPS).astype(BF16)
    dn_ref[...] = _dot(h, wdn_ref[...])
    da = _dot(h, wda_ref[...])
    da_ref[:, :MIX_W] = (da[:, :MIX_W] * (LOG2_E * DA_DQK ** -0.5)).astype(BF16)
    da_ref[:, MIX_W:] = da[:, MIX_W:].astype(BF16)
    raw = _dot(h, wsm_ref[...])
    lane = lax.broadcasted_iota(jnp.int32, raw.shape, 1)
    t = raw + dtb_ref[...]
    softplus = jnp.maximum(t, 0.0) + jnp.log1p(jnp.exp(-jnp.abs(t)))
    g = -jnp.exp(alog_ref[...]) * softplus
    gb_ref[...] = jnp.where(lane < 2 * HEADS, jax.nn.sigmoid(raw), jnp.where(lane < 4 * HEADS, g, 0.0))


def _inproj(x, nw, wdn, wda, wsm, alog, dtb):
    m = x.shape[0]
    tok = lambda w: pl.BlockSpec((TOK_TILE, w), lambda i: (i, 0))
    return pl.pallas_call(
        _inproj_kernel,
        grid=(m // TOK_TILE,),
        in_specs=[tok(D_MODEL), _const_spec((1, D_MODEL)), _const_spec((D_MODEL, 4 * MIX_W)),
                  _const_spec((D_MODEL, 3 * MIX_W)), _const_spec((D_MODEL, LANES)),
                  _const_spec((1, LANES)), _const_spec((1, LANES))],
        out_specs=[tok(4 * MIX_W), tok(3 * MIX_W), tok(LANES)],
        out_shape=[jax.ShapeDtypeStruct((m, 4 * MIX_W), F32), jax.ShapeDtypeStruct((m, 3 * MIX_W), BF16),
                   jax.ShapeDtypeStruct((m, LANES), F32)],
        compiler_params=_params(1),
        name="inproj",
    )(x, nw, wdn, wda, wsm, alog, dtb)


def _dn_kernel(q_ref, k_ref, v_ref, z_ref, cwq_ref, cwk_ref, cwv_ref, gb_ref, nw_ref, o_ref,
               pad_s, qb_s, kb_s, kt_s, kf_s, vf_s, gcol_s, bcol_s, psi_s, m_s, egt_s, of_s, ob_s):
    head = pl.program_id(1)
    pad_s[:CONV_PAD, :] = jnp.zeros((CONV_PAD, HEAD_W), F32)
    pad_s[CONV_PAD + SEQ:, :] = jnp.zeros((CONV_PAD, HEAD_W), F32)

    def conv_silu(x_ref, cw_ref):
        pad_s[CONV_PAD:CONV_PAD + SEQ, :] = x_ref[0]
        first = CONV_PAD - CONV_K // 2
        y = cw_ref[0:1, :] * pad_s[first:first + SEQ, :]
        for j in range(1, CONV_K):
            y = y + cw_ref[j:j + 1, :] * pad_s[first + j:first + j + SEQ, :]
        return _silu(y)

    def l2n(x):
        return x * lax.rsqrt(jnp.sum(x * x, axis=-1, keepdims=True) + 1e-6)

    qn = l2n(conv_silu(q_ref, cwq_ref)) * (HEAD_W ** -0.5)
    kn = l2n(conv_silu(k_ref, cwk_ref))
    qb_s[...] = qn.astype(BF16)
    kb_s[...] = kn.astype(BF16)
    kf_s[...] = kn
    vf_s[...] = conv_silu(v_ref, cwv_ref)
    ri = lax.broadcasted_iota(jnp.int32, (CHUNK, CHUNK), 0)
    ci = lax.broadcasted_iota(jnp.int32, (CHUNK, CHUNK), 1)
    eye = (ri == ci).astype(F32)
    eye_b = eye.astype(BF16)
    for n in range(N_CHUNK):
        kt_s[n] = _dot_nt(eye_b, kb_s[n * CHUNK:(n + 1) * CHUNK, :]).astype(BF16)

    gb = gb_ref[0]
    lane = lax.broadcasted_iota(jnp.int32, (SEQ, LANES), 1)
    for d in range(2):
        beta = jnp.sum(jnp.where(lane == d * HEADS + head, gb, 0.0), axis=-1, keepdims=True)
        g = jnp.sum(jnp.where(lane == 2 * HEADS + d * HEADS + head, gb, 0.0), axis=-1, keepdims=True)
        bcol_s[d] = jnp.broadcast_to(beta, (SEQ, LANES))
        gcol_s[d] = jnp.broadcast_to(g, (SEQ, LANES))

    incl = (ri >= ci, ri <= ci)
    strict = (ri > ci, ri < ci)
    last_row = (CHUNK - 1, 0)
    same = lambda sh: (ri >> sh) == (ci >> sh)
    levels = [same(sh + 1) & ~same(sh) for sh in range(N_LEVELS)]

    def total(gc, d):
        return jnp.broadcast_to(gc[last_row[d]:last_row[d] + 1, :], (CHUNK, CHUNK))

    tri = [m.astype(BF16) for m in incl]

    def prep(i, carry):
        ns = [i * PREP_CHUNKS + c for c in range(PREP_CHUNKS)]
        rows = [pl.ds(pl.multiple_of(n * CHUNK, CHUNK), CHUNK) for n in ns]
        chains = [(c, d) for c in range(PREP_CHUNKS) for d in range(2)]
        kk = [_dot(kb_s[r, :], kt_s[n]) for n, r in zip(ns, rows)]
        qk = [_dot(qb_s[r, :], kt_s[n]) for n, r in zip(ns, rows)]
        gc = {}
        for c, d in chains:
            g = gcol_s[d, rows[c], :]
            hi = g.astype(BF16)
            rem = g - hi.astype(F32)
            mid = rem.astype(BF16)
            lo = (rem - mid.astype(F32)).astype(BF16)
            gc[c, d] = _dot(tri[d], hi) + _dot(tri[d], mid) + _dot(tri[d], lo)
        low, t, a = {}, {}, {}
        for c, d in chains:
            decay = jnp.exp(jnp.where(incl[d], gc[c, d] - gc[c, d].T, -1e30))
            a[c, d] = (qk[c] * decay).astype(BF16)
            low[c, d] = jnp.where(strict[d], kk[c] * bcol_s[d, rows[c], :] * decay, 0.0)
            t[c, d] = eye - jnp.where(levels[0], low[c, d], 0.0)
        for lvl in levels[1:]:
            tb = {k: v.astype(BF16) for k, v in t.items()}
            x = {k: _dot(jnp.where(lvl, low[k], 0.0).astype(BF16), tb[k]).astype(BF16) for k in chains}
            t = {k: t[k] - _dot(tb[k], x[k]) for k in chains}
        uw = {}
        for c, d in chains:
            beta = bcol_s[d, rows[c], :]
            rhs = jnp.concatenate([vf_s[rows[c], :] * beta, kf_s[rows[c], :] * (beta * jnp.exp(gc[c, d]))], axis=1)
            uw[c, d] = _dot(t[c, d].astype(BF16), rhs.astype(BF16))
        au, pw = {}, {}
        for c, d in chains:
            gt = total(gc[c, d], d)
            erc = jnp.exp(gt - gc[c, d])
            scaled = jnp.concatenate([uw[c, d][:, :HEAD_W] * erc, uw[c, d][:, HEAD_W:] * erc], axis=1)
            au[c, d] = _dot(a[c, d], uw[c, d].astype(BF16))
            pw[c, d] = _dot(kt_s[ns[c]], scaled.astype(BF16))
            egt_s[d, ns[c]] = jnp.exp(gt[:8, :])
        for c, d in chains:
            (of_s, ob_s)[d][rows[c], :] = au[c, d][:, :HEAD_W]
            psi_s[d, rows[c], :] = pw[c, d][:, :HEAD_W]
            m_s[d, ns[c], :CHUNK, :] = pw[c, d][:, HEAD_W:].astype(BF16)
            qe = qb_s[rows[c], :].astype(F32) * jnp.exp(gc[c, d])
            m_s[d, ns[c], CHUNK:, :] = (qe - au[c, d][:, HEAD_W:]).astype(BF16)
        return carry

    lax.fori_loop(0, N_CHUNK // PREP_CHUNKS, prep, 0)

    def scan(step, states):
        new = []
        for d in range(2):
            n = step if d == 0 else N_CHUNK - 1 - step
            r = pl.ds(pl.multiple_of(n * CHUNK, CHUNK), CHUNK)
            prod = _dot(m_s[d, n], states[d].astype(BF16))
            o_ref_d = (of_s, ob_s)[d]
            o_ref_d[r, :] = o_ref_d[r, :] + prod[CHUNK:, :]
            new.append(states[d] * egt_s[d, n][0:1, :] - prod[:CHUNK, :] + psi_s[d, r, :])
        return tuple(new)

    zero = jnp.zeros((HEAD_W, HEAD_W), F32)
    lax.fori_loop(0, N_CHUNK, scan, (zero, zero))

    o = _rms(of_s[...] + ob_s[...], nw_ref[...], NORM_EPS) * _silu(z_ref[0])
    o_ref[0] = o.astype(BF16)


def _deltanet(dn, conv_w, gb, norm_w):
    b = dn.shape[0]
    blk = lambda off: pl.BlockSpec((1, SEQ, HEAD_W), lambda i, h: (i, 0, off + h))
    cw = lambda off: pl.BlockSpec((CONV_K, HEAD_W), lambda i, h: (0, off + h))
    seq_f32 = pltpu.VMEM((SEQ, HEAD_W), F32)
    seq_bf16 = pltpu.VMEM((SEQ, HEAD_W), BF16)
    chunks_bf16 = pltpu.VMEM((N_CHUNK, CHUNK, CHUNK), BF16)
    return pl.pallas_call(
        _dn_kernel,
        grid=(b, HEADS),
        in_specs=[blk(0), blk(HEADS), blk(2 * HEADS), blk(3 * HEADS), cw(0), cw(HEADS), cw(2 * HEADS),
                  pl.BlockSpec((1, SEQ, LANES), lambda i, h: (i, 0, 0)), _const_spec((1, HEAD_W))],
        out_specs=pl.BlockSpec((1, SEQ, HEAD_W), lambda i, h: (i, 0, h)),
        out_shape=jax.ShapeDtypeStruct((b, SEQ, MIX_W), BF16),
        scratch_shapes=[pltpu.VMEM((SEQ + 2 * CONV_PAD, HEAD_W), F32),
                        seq_bf16, seq_bf16, chunks_bf16, seq_f32, seq_f32,
                        pltpu.VMEM((2, SEQ, LANES), F32), pltpu.VMEM((2, SEQ, LANES), F32),
                        pltpu.VMEM((2, SEQ, HEAD_W), F32), pltpu.VMEM((2, N_CHUNK, 2 * CHUNK, HEAD_W), BF16),
                        pltpu.VMEM((2, N_CHUNK, 8, LANES), F32), seq_f32, seq_f32],
        compiler_params=_params(2),
        name="deltanet",
    )(dn, dn, dn, dn, conv_w, conv_w, conv_w, gb, norm_w)


def _da_kernel(q_ref, k_ref, v_ref, tab_ref, lp_ref, nw_ref, o_ref, *, lambda_init):
    lp = lp_ref[...]
    lam = (jnp.exp(jnp.sum(lp[0:1, :] * lp[1:2, :], axis=-1, keepdims=True))
           - jnp.exp(jnp.sum(lp[2:3, :] * lp[3:4, :], axis=-1, keepdims=True)) + lambda_init)
    k = k_ref[0]
    v = v_ref[0]
    lane = lax.broadcasted_iota(jnp.int32, (Q_TILE, HEAD_W), 1)

    def scores(qt, m):
        q = q_ref[0, qt * Q_TILE:(qt + 1) * Q_TILE, :]
        qm = jnp.where((lane >= m * DA_DQK) & (lane < (m + 1) * DA_DQK), q, jnp.zeros_like(q))
        start = (N_QT - 1 - qt) * Q_TILE
        return _dot_nt(qm, k) + tab_ref[0, :, start:start + SEQ]

    units = [(qt, m) for qt in range(N_QT) for m in range(2)]
    ahead = [scores(*u) for u in units[:SCORE_LOOKAHEAD]]
    outs = []
    for i, (qt, m) in enumerate(units):
        s = ahead.pop(0)
        if i + SCORE_LOOKAHEAD < len(units):
            ahead.append(scores(*units[i + SCORE_LOOKAHEAD]))
        p = jnp.exp2(s - jnp.max(s, axis=-1, keepdims=True))
        outs.append(_dot(p.astype(BF16), v) / jnp.sum(p, axis=-1, keepdims=True))
        if m == 1:
            o = outs[0] - lam * outs[1]
            outs = []
            o = _rms(o, nw_ref[...], SUBLN_EPS) * (1.0 - lambda_init)
            o_ref[0, qt * Q_TILE:(qt + 1) * Q_TILE, :] = o.astype(BF16)


def _diff_attention(da, table, lam_params, subln_w, lambda_init):
    b = da.shape[0]
    blk = lambda off: pl.BlockSpec((1, SEQ, HEAD_W), lambda h, i: (i, 0, off + h))
    tab_w = (2 * N_QT - 1) * Q_TILE
    return pl.pallas_call(
        functools.partial(_da_kernel, lambda_init=lambda_init),
        grid=(HEADS, b),
        in_specs=[blk(0), blk(HEADS), blk(2 * HEADS),
                  pl.BlockSpec((1, Q_TILE, tab_w), lambda h, i: (h, 0, 0)),
                  _const_spec((4, DA_DQK)), _const_spec((1, HEAD_W))],
        out_specs=pl.BlockSpec((1, SEQ, HEAD_W), lambda h, i: (i, 0, h)),
        out_shape=jax.ShapeDtypeStruct((b, SEQ, MIX_W), BF16),
        compiler_params=_params(2),
        name="diff_attention",
    )(da, da, da, table, lam_params, subln_w)


def _alibi_table():
    slopes = 2.0 ** (-8.0 * jnp.arange(1, HEADS + 1, dtype=F32) / HEADS)
    i = jnp.arange(Q_TILE, dtype=jnp.int32)[:, None]
    c = jnp.arange((2 * N_QT - 1) * Q_TILE, dtype=jnp.int32)[None, :] - (N_QT - 1) * Q_TILE
    return (-LOG2_E * slopes)[:, None, None] * jnp.abs(i - c).astype(F32)[None]


def _merge_kernel(x_ref, dn_ref, da_ref, nw_ref, wgate_ref, wbdn_ref, wbda_ref, wout_ref, o_ref):
    x = x_ref[...]
    h = _rms(x, nw_ref[...], NORM_EPS).astype(BF16)
    gates = jax.nn.sigmoid(_dot(h, wgate_ref[...]))
    y_dn = _dot(dn_ref[...], wbdn_ref[...])
    y_da = _dot(da_ref[...], wbda_ref[...])
    merged = gates[:, :D_MODEL] * y_dn + gates[:, D_MODEL:] * y_da
    o_ref[...] = x + _dot(merged.astype(BF16), wout_ref[...])


def _merge(x, o_dn, o_da, nw, wgate, wbdn, wbda, wout):
    m = x.shape[0]
    tok = lambda w: pl.BlockSpec((TOK_TILE, w), lambda i: (i, 0))
    return pl.pallas_call(
        _merge_kernel,
        grid=(m // TOK_TILE,),
        in_specs=[tok(D_MODEL), tok(MIX_W), tok(MIX_W), _const_spec((1, D_MODEL)),
                  _const_spec((D_MODEL, N_GATE)), _const_spec((MIX_W, D_MODEL)),
                  _const_spec((MIX_W, D_MODEL)), _const_spec((D_MODEL, D_MODEL))],
        out_specs=tok(D_MODEL),
        out_shape=jax.ShapeDtypeStruct((m, D_MODEL), F32),
        compiler_params=_params(1),
        name="merge",
    )(x, o_dn, o_da, nw, wgate, wbdn, wbda, wout)


def _ff_tiles(w):
    return w.reshape(D_MODEL, N_FF, FF_TILE).transpose(1, 0, 2).astype(BF16)


def _lane_row(v):
    return jnp.zeros((1, LANES), F32).at[0, 2 * HEADS:4 * HEADS].set(v.reshape(-1).astype(F32))


def kernel(x_prompt, x_sample, ffn1_norm, ffn1_wg, ffn1_wu, ffn1_wd, mix_norm, w_in, conv_w, dn_a_log, dn_dt_bias, dn_out_norm, diff_lambda, diff_subln, w_branch_dn, w_branch_da, w_out, ffn2_norm, ffn2_wg, ffn2_wu, ffn2_wd, final_norm):
    n_prompt = x_prompt.shape[0]
    batch = n_prompt + x_sample.shape[0]
    x = jnp.concatenate([x_prompt, x_sample], axis=0).reshape(batch * SEQ, D_MODEL)
    table = _alibi_table()
    row = lambda v: v.reshape(1, -1).astype(F32)
    c_dn, c_sm, c_da, c_gate = 4 * MIX_W, 4 * MIX_W + 4 * HEADS, 4 * MIX_W + 4 * HEADS + 3 * MIX_W, w_in.shape[-1]
    for l in range(DEPTH):
        x = _ffn(x, row(ffn1_norm[l]), _ff_tiles(ffn1_wg[l]), _ff_tiles(ffn1_wu[l]), ffn1_wd[l].astype(BF16))
        wl = w_in[l]
        wsm = jnp.zeros((D_MODEL, LANES), BF16).at[:, :4 * HEADS].set(wl[:, c_dn:c_sm].astype(BF16))
        dn, da, gb = _inproj(x, row(mix_norm[l]), wl[:, :c_dn].astype(BF16), wl[:, c_sm:c_da].astype(BF16), wsm,
                             _lane_row(dn_a_log[l]), _lane_row(dn_dt_bias[l]))
        o_dn = _deltanet(dn.reshape(batch, SEQ, 4 * MIX_W), conv_w[l].astype(F32),
                         gb.reshape(batch, SEQ, LANES), row(dn_out_norm[l]))
        lambda_init = 0.8 - 0.6 * math.exp(-0.3 * l)
        o_da = _diff_attention(da.reshape(batch, SEQ, 3 * MIX_W), table, diff_lambda[l].astype(F32),
                               row(diff_subln[l]), lambda_init)
        x = _merge(x, o_dn.reshape(batch * SEQ, MIX_W), o_da.reshape(batch * SEQ, MIX_W), row(mix_norm[l]),
                   wl[:, c_da:c_gate].astype(BF16), w_branch_dn[l].astype(BF16), w_branch_da[l].astype(BF16),
                   w_out[l].astype(BF16))
        x = _ffn(x, row(ffn2_norm[l]), _ff_tiles(ffn2_wg[l]), _ff_tiles(ffn2_wu[l]), ffn2_wd[l].astype(BF16),
                 final_w=row(final_norm) if l == DEPTH - 1 else None)
    y = x.reshape(batch, SEQ, D_MODEL)
    return (y[:n_prompt], y[n_prompt:])
```

```python
import functools
import math

import jax
import jax.numpy as jnp
from jax import lax
from jax.experimental import pallas as pl
from jax.experimental.pallas import tpu as pltpu

F32 = jnp.float32
BF16 = jnp.bfloat16

D_MODEL = 1024
SEQ = 2048
DEPTH = 4
D_FF = 2816
HEADS = 4
HEAD_W = 128
DA_DQK = 64
MIX_W = HEADS * HEAD_W
N_GATE = 2 * D_MODEL
NORM_EPS = 1e-6
SUBLN_EPS = 1e-5
CONV_K = 5
CONV_PAD = 8

LOG2_E = math.log2(math.e)
LANES = 128
MXU_N = 256
CHUNK = 128
N_CHUNK = SEQ // CHUNK
N_LEVELS = 7
Q_TILE = 256
N_QT = SEQ // Q_TILE
SCORE_LOOKAHEAD = 1
TOK_TILE = 512
FF_TILE = MXU_N
N_FF = D_FF // FF_TILE
VMEM_LIMIT = 56 * 1024 * 1024


def _rms(x, w, eps):
    return x * lax.rsqrt(jnp.mean(x * x, axis=-1, keepdims=True) + eps) * w


def _silu(x):
    return x * jax.nn.sigmoid(x)


def _dot(a, b):
    return jnp.dot(a, b, preferred_element_type=F32)


def _dot_nt(a, b):
    return lax.dot_general(a, b, (((1,), (1,)), ((), ())), preferred_element_type=F32)


def _const_spec(shape):
    zeros = (0,) * len(shape)
    return pl.BlockSpec(shape, lambda *_: zeros, pipeline_mode=pl.Buffered(1))


def _params(n_grid):
    return pltpu.CompilerParams(dimension_semantics=("arbitrary",) * n_grid, vmem_limit_bytes=VMEM_LIMIT)


def _ffn_kernel(*refs, final, n_first):
    refs = list(refs)
    x_ref = refs.pop(0)
    x2_ref = refs.pop(0) if n_first else None
    nw_ref, wg_ref, wu_ref, wd_ref = refs[:4]
    fw_ref = refs[4] if final else None
    o_ref, h_scr, a_scr = refs[-3:]
    x = x_ref[...]
    if n_first:
        x = jnp.where(pl.program_id(0) < n_first, x, x2_ref[...])
    h_scr[...] = _rms(x, nw_ref[...], NORM_EPS).astype(BF16)
    for j in range(N_FF):
        h = h_scr[...]
        cols = slice(j * FF_TILE, (j + 1) * FF_TILE)
        g = _dot(h, wg_ref[:, cols])
        u = _dot(h, wu_ref[:, cols])
        a_scr[:, cols] = (_silu(g) * u).astype(BF16)
    out = x + 0.5 * _dot(a_scr[...], wd_ref[...])
    if final:
        out = _rms(out, fw_ref[...], NORM_EPS)
    o_ref[...] = out


def _ffn(x, nw, wg, wu, wd, final_w=None, x2=None):
    n_first = 0 if x2 is None else x.shape[0] // TOK_TILE
    m = x.shape[0] + (0 if x2 is None else x2.shape[0])
    tok = pl.BlockSpec((TOK_TILE, D_MODEL), lambda i: (i, 0))
    in_specs, args = [tok], [x]
    if x2 is not None:
        in_specs = [pl.BlockSpec((TOK_TILE, D_MODEL), lambda i: (jnp.minimum(i, n_first - 1), 0)),
                    pl.BlockSpec((TOK_TILE, D_MODEL), lambda i: (jnp.maximum(i - n_first, 0), 0))]
        args = [x, x2]
    in_specs += [_const_spec((1, D_MODEL)), _const_spec((D_MODEL, D_FF)), _const_spec((D_MODEL, D_FF)),
                 _const_spec((D_FF, D_MODEL))]
    args += [nw, wg, wu, wd]
    if final_w is not None:
        in_specs.append(_const_spec((1, D_MODEL)))
        args.append(final_w)
    return pl.pallas_call(
        functools.partial(_ffn_kernel, final=final_w is not None, n_first=n_first),
        grid=(m // TOK_TILE,),
        in_specs=in_specs,
        out_specs=tok,
        out_shape=jax.ShapeDtypeStruct((m, D_MODEL), F32),
        scratch_shapes=[pltpu.VMEM((TOK_TILE, D_MODEL), BF16), pltpu.VMEM((TOK_TILE, D_FF), BF16)],
        compiler_params=_params(1),
        name="ffn_final" if final_w is not None else ("ffn_first" if x2 is not None else "ffn"),
    )(*args)


def _inproj_kernel(x_ref, nw_ref, wdn_ref, wda_ref, wsm_ref, alog_ref, dtb_ref, dn_ref, da_ref, gb_ref):
    h = _rms(x_ref[...], nw_ref[...], NORM_EPS).astype(BF16)
    dn_ref[...] = _dot(h, wdn_ref[...])
    da = _dot(h, wda_ref[...])
    da_ref[:, :MIX_W] = (da[:, :MIX_W] * (LOG2_E * DA_DQK ** -0.5)).astype(BF16)
    da_ref[:, MIX_W:] = da[:, MIX_W:].astype(BF16)
    raw = _dot(h, wsm_ref[...])
    lane = lax.broadcasted_iota(jnp.int32, raw.shape, 1)
    t = raw + dtb_ref[...]
    softplus = jnp.maximum(t, 0.0) + jnp.log1p(jnp.exp(-jnp.abs(t)))
    g = -jnp.exp(alog_ref[...]) * softplus
    gb_ref[...] = jnp.where(lane < 2 * HEADS, jax.nn.sigmoid(raw), jnp.where(lane < 4 * HEADS, g, 0.0))


def _inproj(x, nw, wdn, wda, wsm, alog, dtb):
    m = x.shape[0]
    tok = lambda w: pl.BlockSpec((TOK_TILE, w), lambda i: (i, 0))
    return pl.pallas_call(
        _inproj_kernel,
        grid=(m // TOK_TILE,),
        in_specs=[tok(D_MODEL), _const_spec((1, D_MODEL)), _const_spec((D_MODEL, 4 * MIX_W)),
                  _const_spec((D_MODEL, 3 * MIX_W)), _const_spec((D_MODEL, LANES)),
                  _const_spec((1, LANES)), _const_spec((1, LANES))],
        out_specs=[tok(4 * MIX_W), tok(3 * MIX_W), tok(LANES)],
        out_shape=[jax.ShapeDtypeStruct((m, 4 * MIX_W), F32), jax.ShapeDtypeStruct((m, 3 * MIX_W), BF16),
                   jax.ShapeDtypeStruct((m, LANES), F32)],
        compiler_params=_params(1),
        name="inproj",
    )(x, nw, wdn, wda, wsm, alog, dtb)


def _dn_kernel(q_ref, k_ref, v_ref, z_ref, cwq_ref, cwk_ref, cwv_ref, gb_ref, nw_ref, o_ref,
               pad_s, qb_s, kb_s, kt_s, kf_s, vf_s, gcol_s, bcol_s, psi_s, m_s, egt_s, of_s, ob_s, *, n_items):
    step = pl.program_id(0)
    head = jnp.minimum(step, n_items - 1) % HEADS
    wr = step % 2
    rd = 1 - wr

    @pl.when(step == 0)
    def _():
        psi_s[1] = jnp.zeros(psi_s.shape[1:], F32)
        m_s[1] = jnp.zeros(m_s.shape[1:], BF16)
        egt_s[1] = jnp.zeros(egt_s.shape[1:], F32)
        of_s[1] = jnp.zeros(of_s.shape[1:], F32)
        ob_s[1] = jnp.zeros(ob_s.shape[1:], F32)

    states = [jnp.zeros((HEAD_W, HEAD_W), F32), jnp.zeros((HEAD_W, HEAD_W), F32)]
    issued = [0]

    def recurrence_step():
        t = issued[0]
        if t == N_CHUNK:
            return
        issued[0] = t + 1
        for d in range(2):
            n = t if d == 0 else N_CHUNK - 1 - t
            r = slice(n * CHUNK, (n + 1) * CHUNK)
            prod = _dot(m_s[rd, d, n], states[d].astype(BF16))
            o_d = (of_s, ob_s)[d]
            o_d[rd, r, :] = o_d[rd, r, :] + prod[CHUNK:, :]
            states[d] = states[d] * egt_s[rd, d, n][0:1, :] - prod[:CHUNK, :] + psi_s[rd, d, r, :]

    pad_s[:CONV_PAD, :] = jnp.zeros((CONV_PAD, HEAD_W), F32)
    pad_s[CONV_PAD + SEQ:, :] = jnp.zeros((CONV_PAD, HEAD_W), F32)

    def conv_silu(x_ref, cw_ref):
        pad_s[CONV_PAD:CONV_PAD + SEQ, :] = x_ref[0]
        first = CONV_PAD - CONV_K // 2
        y = cw_ref[0:1, :] * pad_s[first:first + SEQ, :]
        for j in range(1, CONV_K):
            y = y + cw_ref[j:j + 1, :] * pad_s[first + j:first + j + SEQ, :]
        return _silu(y)

    def l2n(x):
        return x * lax.rsqrt(jnp.sum(x * x, axis=-1, keepdims=True) + 1e-6)

    qn = l2n(conv_silu(q_ref, cwq_ref)) * (HEAD_W ** -0.5)
    qb_s[...] = qn.astype(BF16)
    recurrence_step()
    kn = l2n(conv_silu(k_ref, cwk_ref))
    kb_s[...] = kn.astype(BF16)
    kf_s[...] = kn
    recurrence_step()
    vf_s[...] = conv_silu(v_ref, cwv_ref)
    recurrence_step()
    ri = lax.broadcasted_iota(jnp.int32, (CHUNK, CHUNK), 0)
    ci = lax.broadcasted_iota(jnp.int32, (CHUNK, CHUNK), 1)
    eye = (ri == ci).astype(F32)
    eye_b = eye.astype(BF16)
    for n in range(N_CHUNK):
        kt_s[n] = _dot_nt(eye_b, kb_s[n * CHUNK:(n + 1) * CHUNK, :]).astype(BF16)
    recurrence_step()

    gb = gb_ref[0]
    lane = lax.broadcasted_iota(jnp.int32, (SEQ, LANES), 1)
    for d in range(2):
        beta = jnp.sum(jnp.where(lane == d * HEADS + head, gb, 0.0), axis=-1, keepdims=True)
        g = jnp.sum(jnp.where(lane == 2 * HEADS + d * HEADS + head, gb, 0.0), axis=-1, keepdims=True)
        bcol_s[d] = jnp.broadcast_to(beta, (SEQ, LANES))
        gcol_s[d] = jnp.broadcast_to(g, (SEQ, LANES))
        recurrence_step()

    incl = (ri >= ci, ri <= ci)
    strict = (ri > ci, ri < ci)
    last_row = (CHUNK - 1, 0)
    same = lambda sh: (ri >> sh) == (ci >> sh)
    levels = [same(sh + 1) & ~same(sh) for sh in range(N_LEVELS)]

    def total(gc, d):
        return jnp.broadcast_to(gc[last_row[d]:last_row[d] + 1, :], (CHUNK, CHUNK))

    tri = [m.astype(BF16) for m in incl]

    rows = [slice(n * CHUNK, (n + 1) * CHUNK) for n in range(N_CHUNK)]
    chains = [(c, d) for c in range(N_CHUNK) for d in range(2)]
    kk = [_dot(kb_s[r, :], kt_s[n]) for n, r in enumerate(rows)]
    qk = [_dot(qb_s[r, :], kt_s[n]) for n, r in enumerate(rows)]
    recurrence_step()
    gc = {}
    for c, d in chains:
        g = gcol_s[d, rows[c], :]
        hi = g.astype(BF16)
        rem = g - hi.astype(F32)
        mid = rem.astype(BF16)
        lo = (rem - mid.astype(F32)).astype(BF16)
        gc[c, d] = _dot(tri[d], hi) + _dot(tri[d], mid) + _dot(tri[d], lo)
    recurrence_step()
    low, t, a = {}, {}, {}
    for c, d in chains:
        decay = jnp.exp(jnp.where(incl[d], gc[c, d] - gc[c, d].T, -1e30))
        a[c, d] = (qk[c] * decay).astype(BF16)
        low[c, d] = jnp.where(strict[d], kk[c] * bcol_s[d, rows[c], :] * decay, 0.0)
        t[c, d] = eye - jnp.where(levels[0], low[c, d], 0.0)
    recurrence_step()
    for lvl in levels[1:]:
        tb = {k: v.astype(BF16) for k, v in t.items()}
        x = {k: _dot(jnp.where(lvl, low[k], 0.0).astype(BF16), tb[k]).astype(BF16) for k in chains}
        recurrence_step()
        t = {k: t[k] - _dot(tb[k], x[k]) for k in chains}
        recurrence_step()
    uw = {}
    for c, d in chains:
        beta = bcol_s[d, rows[c], :]
        rhs = jnp.concatenate([vf_s[rows[c], :] * beta, kf_s[rows[c], :] * (beta * jnp.exp(gc[c, d]))], axis=1)
        uw[c, d] = _dot(t[c, d].astype(BF16), rhs.astype(BF16))
    au, pw = {}, {}
    for c, d in chains:
        erc = jnp.exp(total(gc[c, d], d) - gc[c, d])
        scaled = jnp.concatenate([uw[c, d][:, :HEAD_W] * erc, uw[c, d][:, HEAD_W:] * erc], axis=1)
        au[c, d] = _dot(a[c, d], uw[c, d].astype(BF16))
        pw[c, d] = _dot(kt_s[c], scaled.astype(BF16))
    while issued[0] < N_CHUNK:
        recurrence_step()
    o = _rms(of_s[rd] + ob_s[rd], nw_ref[...], NORM_EPS) * _silu(z_ref[0])
    o_ref[0] = o.astype(BF16)
    for c, d in chains:
        (of_s, ob_s)[d][wr, rows[c], :] = au[c, d][:, :HEAD_W]
        psi_s[wr, d, rows[c], :] = pw[c, d][:, :HEAD_W]
        m_s[wr, d, c, :CHUNK, :] = pw[c, d][:, HEAD_W:].astype(BF16)
        qe = qb_s[rows[c], :].astype(F32) * jnp.exp(gc[c, d])
        m_s[wr, d, c, CHUNK:, :] = (qe - au[c, d][:, HEAD_W:]).astype(BF16)
        egt_s[wr, d, c] = jnp.exp(total(gc[c, d], d)[:8, :])


def _deltanet(dn, conv_w, gb, norm_w):
    b = dn.shape[0]
    n_items = b * HEADS
    prep_item = lambda s: jnp.minimum(s, n_items - 1)
    scan_item = lambda s: jnp.maximum(s - 1, 0)
    blk = lambda off, item: pl.BlockSpec((1, SEQ, HEAD_W), lambda s: (item(s) // HEADS, 0, off + item(s) % HEADS))
    cw = lambda off: pl.BlockSpec((CONV_K, HEAD_W), lambda s: (0, off + prep_item(s) % HEADS))
    seq_f32 = pltpu.VMEM((SEQ, HEAD_W), F32)
    seq_bf16 = pltpu.VMEM((SEQ, HEAD_W), BF16)
    chunks_bf16 = pltpu.VMEM((N_CHUNK, CHUNK, CHUNK), BF16)
    return pl.pallas_call(
        functools.partial(_dn_kernel, n_items=n_items),
        grid=(n_items + 1,),
        in_specs=[blk(0, prep_item), blk(HEADS, prep_item), blk(2 * HEADS, prep_item), blk(3 * HEADS, scan_item),
                  cw(0), cw(HEADS), cw(2 * HEADS),
                  pl.BlockSpec((1, SEQ, LANES), lambda s: (prep_item(s) // HEADS, 0, 0)), _const_spec((1, HEAD_W))],
        out_specs=blk(0, scan_item),
        out_shape=jax.ShapeDtypeStruct((b, SEQ, MIX_W), BF16),
        scratch_shapes=[pltpu.VMEM((SEQ + 2 * CONV_PAD, HEAD_W), F32),
                        seq_bf16, seq_bf16, chunks_bf16, seq_f32, seq_f32,
                        pltpu.VMEM((2, SEQ, LANES), F32), pltpu.VMEM((2, SEQ, LANES), F32),
                        pltpu.VMEM((2, 2, SEQ, HEAD_W), F32), pltpu.VMEM((2, 2, N_CHUNK, 2 * CHUNK, HEAD_W), BF16),
                        pltpu.VMEM((2, 2, N_CHUNK, 8, LANES), F32),
                        pltpu.VMEM((2, SEQ, HEAD_W), F32), pltpu.VMEM((2, SEQ, HEAD_W), F32)],
        compiler_params=_params(1),
        name="deltanet",
    )(dn, dn, dn, dn, conv_w, conv_w, conv_w, gb, norm_w)


def _da_kernel(q_ref, k_ref, v_ref, tab_ref, lp_ref, nw_ref, o_ref, *, lambda_init):
    lp = lp_ref[...]
    lam = (jnp.exp(jnp.sum(lp[0:1, :] * lp[1:2, :], axis=-1, keepdims=True))
           - jnp.exp(jnp.sum(lp[2:3, :] * lp[3:4, :], axis=-1, keepdims=True)) + lambda_init)
    k = k_ref[0]
    v = v_ref[0]
    lane = lax.broadcasted_iota(jnp.int32, (Q_TILE, HEAD_W), 1)

    def scores(qt, m):
        q = q_ref[0, qt * Q_TILE:(qt + 1) * Q_TILE, :]
        qm = jnp.where((lane >= m * DA_DQK) & (lane < (m + 1) * DA_DQK), q, jnp.zeros_like(q))
        start = (N_QT - 1 - qt) * Q_TILE
        return _dot_nt(qm, k) + tab_ref[0, :, start:start + SEQ]

    units = [(qt, m) for qt in range(N_QT) for m in range(2)]
    ahead = [scores(*u) for u in units[:SCORE_LOOKAHEAD]]
    outs = []
    for i, (qt, m) in enumerate(units):
        s = ahead.pop(0)
        if i + SCORE_LOOKAHEAD < len(units):
            ahead.append(scores(*units[i + SCORE_LOOKAHEAD]))
        p = jnp.exp2(s - jnp.max(s, axis=-1, keepdims=True))
        outs.append(_dot(p.astype(BF16), v) / jnp.sum(p, axis=-1, keepdims=True))
        if m == 1:
            o = outs[0] - lam * outs[1]
            outs = []
            o = _rms(o, nw_ref[...], SUBLN_EPS) * (1.0 - lambda_init)
            o_ref[0, qt * Q_TILE:(qt + 1) * Q_TILE, :] = o.astype(BF16)


def _diff_attention(da, table, lam_params, subln_w, lambda_init):
    b = da.shape[0]
    blk = lambda off: pl.BlockSpec((1, SEQ, HEAD_W), lambda h, i: (i, 0, off + h))
    tab_w = (2 * N_QT - 1) * Q_TILE
    return pl.pallas_call(
        functools.partial(_da_kernel, lambda_init=lambda_init),
        grid=(HEADS, b),
        in_specs=[blk(0), blk(HEADS), blk(2 * HEADS),
                  pl.BlockSpec((1, Q_TILE, tab_w), lambda h, i: (h, 0, 0)),
                  _const_spec((4, DA_DQK)), _const_spec((1, HEAD_W))],
        out_specs=pl.BlockSpec((1, SEQ, HEAD_W), lambda h, i: (i, 0, h)),
        out_shape=jax.ShapeDtypeStruct((b, SEQ, MIX_W), BF16),
        compiler_params=_params(2),
        name="diff_attention",
    )(da, da, da, table, lam_params, subln_w)


def _alibi_table():
    slopes = 2.0 ** (-8.0 * jnp.arange(1, HEADS + 1, dtype=F32) / HEADS)
    i = jnp.arange(Q_TILE, dtype=jnp.int32)[:, None]
    c = jnp.arange((2 * N_QT - 1) * Q_TILE, dtype=jnp.int32)[None, :] - (N_QT - 1) * Q_TILE
    return (-LOG2_E * slopes)[:, None, None] * jnp.abs(i - c).astype(F32)[None]


def _merge_kernel(x_ref, dn_ref, da_ref, nw_ref, wgate_ref, wbdn_ref, wbda_ref, wout_ref, o_ref):
    x = x_ref[...]
    h = _rms(x, nw_ref[...], NORM_EPS).astype(BF16)
    gates = jax.nn.sigmoid(_dot(h, wgate_ref[...]))
    y_dn = _dot(dn_ref[...], wbdn_ref[...])
    y_da = _dot(da_ref[...], wbda_ref[...])
    merged = gates[:, :D_MODEL] * y_dn + gates[:, D_MODEL:] * y_da
    o_ref[...] = x + _dot(merged.astype(BF16), wout_ref[...])


def _merge(x, o_dn, o_da, nw, wgate, wbdn, wbda, wout):
    m = x.shape[0]
    tok = lambda w: pl.BlockSpec((TOK_TILE, w), lambda i: (i, 0))
    return pl.pallas_call(
        _merge_kernel,
        grid=(m // TOK_TILE,),
        in_specs=[tok(D_MODEL), tok(MIX_W), tok(MIX_W), _const_spec((1, D_MODEL)),
                  _const_spec((D_MODEL, N_GATE)), _const_spec((MIX_W, D_MODEL)),
                  _const_spec((MIX_W, D_MODEL)), _const_spec((D_MODEL, D_MODEL))],
        out_specs=tok(D_MODEL),
        out_shape=jax.ShapeDtypeStruct((m, D_MODEL), F32),
        compiler_params=_params(1),
        name="merge",
    )(x, o_dn, o_da, nw, wgate, wbdn, wbda, wout)


def _lane_row(v):
    return jnp.zeros((1, LANES), F32).at[0, 2 * HEADS:4 * HEADS].set(v.reshape(-1).astype(F32))


def kernel(x_prompt, x_sample, ffn1_norm, ffn1_wg, ffn1_wu, ffn1_wd, mix_norm, w_in, conv_w, dn_a_log, dn_dt_bias, dn_out_norm, diff_lambda, diff_subln, w_branch_dn, w_branch_da, w_out, ffn2_norm, ffn2_wg, ffn2_wu, ffn2_wd, final_norm):
    n_prompt = x_prompt.shape[0]
    batch = n_prompt + x_sample.shape[0]
    x = x_prompt.reshape(n_prompt * SEQ, D_MODEL)
    x2 = x_sample.reshape((batch - n_prompt) * SEQ, D_MODEL)
    table = _alibi_table()
    row = lambda v: v.reshape(1, -1).astype(F32)
    c_dn, c_sm, c_da, c_gate = 4 * MIX_W, 4 * MIX_W + 4 * HEADS, 4 * MIX_W + 4 * HEADS + 3 * MIX_W, w_in.shape[-1]
    for l in range(DEPTH):
        x = _ffn(x, row(ffn1_norm[l]), ffn1_wg[l].astype(BF16), ffn1_wu[l].astype(BF16), ffn1_wd[l].astype(BF16),
                 x2=x2 if l == 0 else None)
        wl = w_in[l]
        wsm = jnp.zeros((D_MODEL, LANES), BF16).at[:, :4 * HEADS].set(wl[:, c_dn:c_sm].astype(BF16))
        dn, da, gb = _inproj(x, row(mix_norm[l]), wl[:, :c_dn].astype(BF16), wl[:, c_sm:c_da].astype(BF16), wsm,
                             _lane_row(dn_a_log[l]), _lane_row(dn_dt_bias[l]))
        o_dn = _deltanet(dn.reshape(batch, SEQ, 4 * MIX_W), conv_w[l].astype(F32),
                         gb.reshape(batch, SEQ, LANES), row(dn_out_norm[l]))
        lambda_init = 0.8 - 0.6 * math.exp(-0.3 * l)
        o_da = _diff_attention(da.reshape(batch, SEQ, 3 * MIX_W), table, diff_lambda[l].astype(F32),
                               row(diff_subln[l]), lambda_init)
        x = _merge(x, o_dn.reshape(batch * SEQ, MIX_W), o_da.reshape(batch * SEQ, MIX_W), row(mix_norm[l]),
                   wl[:, c_da:c_gate].astype(BF16), w_branch_dn[l].astype(BF16), w_branch_da[l].astype(BF16),
                   w_out[l].astype(BF16))
        x = _ffn(x, row(ffn2_norm[l]), ffn2_wg[l].astype(BF16), ffn2_wu[l].astype(BF16), ffn2_wd[l].astype(BF16),
                 final_w=row(final_norm) if l == DEPTH - 1 else None)
    y = x.reshape(batch, SEQ, D_MODEL)
    return (y[:n_prompt], y[n_prompt:])
```

```python
import functools
import math

import jax
import jax.numpy as jnp
from jax import lax
from jax.experimental import pallas as pl
from jax.experimental.pallas import tpu as pltpu

F32 = jnp.float32
BF16 = jnp.bfloat16

D_MODEL = 1024
SEQ = 2048
DEPTH = 4
D_FF = 2816
HEADS = 4
HEAD_W = 128
DA_DQK = 64
MIX_W = HEADS * HEAD_W
N_GATE = 2 * D_MODEL
NORM_EPS = 1e-6
SUBLN_EPS = 1e-5
CONV_K = 5
CONV_PAD = 8

LOG2_E = math.log2(math.e)
LANES = 128
MXU_N = 256
CHUNK = 128
N_CHUNK = SEQ // CHUNK
N_LEVELS = 7
Q_TILE = 256
N_QT = SEQ // Q_TILE
SCORE_LOOKAHEAD = 1
TOK_TILE = 512
FF_TILE = MXU_N
N_FF = D_FF // FF_TILE
VMEM_LIMIT = 56 * 1024 * 1024


def _rms(x, w, eps):
    return x * lax.rsqrt(jnp.mean(x * x, axis=-1, keepdims=True) + eps) * w


def _silu(x):
    return x * jax.nn.sigmoid(x)


def _dot(a, b):
    return jnp.dot(a, b, preferred_element_type=F32)


def _dot_nt(a, b):
    return lax.dot_general(a, b, (((1,), (1,)), ((), ())), preferred_element_type=F32)


def _const_spec(shape):
    zeros = (0,) * len(shape)
    return pl.BlockSpec(shape, lambda *_: zeros, pipeline_mode=pl.Buffered(1))


def _params(n_grid):
    return pltpu.CompilerParams(dimension_semantics=("arbitrary",) * n_grid, vmem_limit_bytes=VMEM_LIMIT)


def _ffn_kernel(*refs, final, n_first):
    refs = list(refs)
    x_ref = refs.pop(0)
    x2_ref = refs.pop(0) if n_first else None
    nw_ref, wg_ref, wu_ref, wd_ref = refs[:4]
    fw_ref = refs[4] if final else None
    o_ref, h_scr, a_scr = refs[-3:]
    x = x_ref[...]
    if n_first:
        x = jnp.where(pl.program_id(0) < n_first, x, x2_ref[...])
    h_scr[...] = _rms(x, nw_ref[...], NORM_EPS).astype(BF16)
    for j in range(N_FF):
        h = h_scr[...]
        cols = slice(j * FF_TILE, (j + 1) * FF_TILE)
        g = _dot(h, wg_ref[:, cols])
        u = _dot(h, wu_ref[:, cols])
        a_scr[:, cols] = (_silu(g) * u).astype(BF16)
    out = x + 0.5 * _dot(a_scr[...], wd_ref[...])
    if final:
        out = _rms(out, fw_ref[...], NORM_EPS)
    o_ref[...] = out


def _ffn(x, nw, wg, wu, wd, final_w=None, x2=None):
    n_first = 0 if x2 is None else x.shape[0] // TOK_TILE
    m = x.shape[0] + (0 if x2 is None else x2.shape[0])
    tok = pl.BlockSpec((TOK_TILE, D_MODEL), lambda i: (i, 0))
    in_specs, args = [tok], [x]
    if x2 is not None:
        in_specs = [pl.BlockSpec((TOK_TILE, D_MODEL), lambda i: (jnp.minimum(i, n_first - 1), 0)),
                    pl.BlockSpec((TOK_TILE, D_MODEL), lambda i: (jnp.maximum(i - n_first, 0), 0))]
        args = [x, x2]
    in_specs += [_const_spec((1, D_MODEL)), _const_spec((D_MODEL, D_FF)), _const_spec((D_MODEL, D_FF)),
                 _const_spec((D_FF, D_MODEL))]
    args += [nw, wg, wu, wd]
    if final_w is not None:
        in_specs.append(_const_spec((1, D_MODEL)))
        args.append(final_w)
    return pl.pallas_call(
        functools.partial(_ffn_kernel, final=final_w is not None, n_first=n_first),
        grid=(m // TOK_TILE,),
        in_specs=in_specs,
        out_specs=tok,
        out_shape=jax.ShapeDtypeStruct((m, D_MODEL), F32),
        scratch_shapes=[pltpu.VMEM((TOK_TILE, D_MODEL), BF16), pltpu.VMEM((TOK_TILE, D_FF), BF16)],
        compiler_params=_params(1),
        name="ffn_final" if final_w is not None else ("ffn_first" if x2 is not None else "ffn"),
    )(*args)


def _inproj_kernel(x_ref, nw_ref, wdn_ref, wda_ref, wsm_ref, alog_ref, dtb_ref, dn_ref, da_ref, gb_ref):
    h = _rms(x_ref[...], nw_ref[...], NORM_EPS).astype(BF16)
    dn_ref[...] = _dot(h, wdn_ref[...])
    da = _dot(h, wda_ref[...])
    da_ref[:, :MIX_W] = (da[:, :MIX_W] * (LOG2_E * DA_DQK ** -0.5)).astype(BF16)
    da_ref[:, MIX_W:] = da[:, MIX_W:].astype(BF16)
    raw = _dot(h, wsm_ref[...])
    lane = lax.broadcasted_iota(jnp.int32, raw.shape, 1)
    t = raw + dtb_ref[...]
    softplus = jnp.maximum(t, 0.0) + jnp.log1p(jnp.exp(-jnp.abs(t)))
    g = -jnp.exp(alog_ref[...]) * softplus
    gb_ref[...] = jnp.where(lane < 2 * HEADS, jax.nn.sigmoid(raw), jnp.where(lane < 4 * HEADS, g, 0.0))


def _inproj(x, nw, wdn, wda, wsm, alog, dtb):
    m = x.shape[0]
    tok = lambda w: pl.BlockSpec((TOK_TILE, w), lambda i: (i, 0))
    return pl.pallas_call(
        _inproj_kernel,
        grid=(m // TOK_TILE,),
        in_specs=[tok(D_MODEL), _const_spec((1, D_MODEL)), _const_spec((D_MODEL, 4 * MIX_W)),
                  _const_spec((D_MODEL, 3 * MIX_W)), _const_spec((D_MODEL, LANES)),
                  _const_spec((1, LANES)), _const_spec((1, LANES))],
        out_specs=[tok(4 * MIX_W), tok(3 * MIX_W), tok(LANES)],
        out_shape=[jax.ShapeDtypeStruct((m, 4 * MIX_W), F32), jax.ShapeDtypeStruct((m, 3 * MIX_W), BF16),
                   jax.ShapeDtypeStruct((m, LANES), F32)],
        compiler_params=_params(1),
        name="inproj",
    )(x, nw, wdn, wda, wsm, alog, dtb)


def _dn_kernel(q_ref, k_ref, v_ref, z_ref, cwq_ref, cwk_ref, cwv_ref, gb_ref, nw_ref, o_ref,
               pad_s, qb_s, kb_s, kt_s, kf_s, vf_s, gcol_s, bcol_s, psi_s, m_s, egt_s, of_s, ob_s, *, n_items):
    step = pl.program_id(0)
    head = jnp.minimum(step, n_items - 1) % HEADS
    wr = step % 2
    rd = 1 - wr

    @pl.when(step == 0)
    def _():
        psi_s[1] = jnp.zeros(psi_s.shape[1:], F32)
        m_s[1] = jnp.zeros(m_s.shape[1:], BF16)
        egt_s[1] = jnp.zeros(egt_s.shape[1:], F32)
        of_s[1] = jnp.zeros(of_s.shape[1:], F32)
        ob_s[1] = jnp.zeros(ob_s.shape[1:], F32)

    states = [jnp.zeros((HEAD_W, HEAD_W), F32), jnp.zeros((HEAD_W, HEAD_W), F32)]
    issued = [0]

    def recurrence_step():
        t = issued[0]
        if t == N_CHUNK:
            return
        issued[0] = t + 1
        for d in range(2):
            n = t if d == 0 else N_CHUNK - 1 - t
            r = slice(n * CHUNK, (n + 1) * CHUNK)
            prod = _dot(m_s[rd, d, n], states[d].astype(BF16))
            o_d = (of_s, ob_s)[d]
            o_d[rd, r, :] = o_d[rd, r, :] + prod[CHUNK:, :]
            states[d] = states[d] * egt_s[rd, d, n][0:1, :] - prod[:CHUNK, :] + psi_s[rd, d, r, :]

    pad_s[:CONV_PAD, :] = jnp.zeros((CONV_PAD, HEAD_W), F32)
    pad_s[CONV_PAD + SEQ:, :] = jnp.zeros((CONV_PAD, HEAD_W), F32)

    def conv_silu(x_ref, cw_ref):
        pad_s[CONV_PAD:CONV_PAD + SEQ, :] = x_ref[0]
        first = CONV_PAD - CONV_K // 2
        y = cw_ref[0:1, :] * pad_s[first:first + SEQ, :]
        for j in range(1, CONV_K):
            y = y + cw_ref[j:j + 1, :] * pad_s[first + j:first + j + SEQ, :]
        return _silu(y)

    def l2n(x):
        return x * lax.rsqrt(jnp.sum(x * x, axis=-1, keepdims=True) + 1e-6)

    qn = l2n(conv_silu(q_ref, cwq_ref)) * (HEAD_W ** -0.5)
    qb_s[...] = qn.astype(BF16)
    recurrence_step()
    kn = l2n(conv_silu(k_ref, cwk_ref))
    kb_s[...] = kn.astype(BF16)
    kf_s[...] = kn
    recurrence_step()
    vf_s[...] = conv_silu(v_ref, cwv_ref)
    recurrence_step()
    ri = lax.broadcasted_iota(jnp.int32, (CHUNK, CHUNK), 0)
    ci = lax.broadcasted_iota(jnp.int32, (CHUNK, CHUNK), 1)
    eye = (ri == ci).astype(F32)
    eye_b = eye.astype(BF16)
    for n in range(N_CHUNK):
        kt_s[n] = _dot_nt(eye_b, kb_s[n * CHUNK:(n + 1) * CHUNK, :]).astype(BF16)
    recurrence_step()

    gb = gb_ref[0]
    lane = lax.broadcasted_iota(jnp.int32, (SEQ, LANES), 1)
    for d in range(2):
        beta = jnp.sum(jnp.where(lane == d * HEADS + head, gb, 0.0), axis=-1, keepdims=True)
        g = jnp.sum(jnp.where(lane == 2 * HEADS + d * HEADS + head, gb, 0.0), axis=-1, keepdims=True)
        bcol_s[d] = jnp.broadcast_to(beta, (SEQ, LANES))
        gcol_s[d] = jnp.broadcast_to(g, (SEQ, LANES))
        recurrence_step()

    incl = (ri >= ci, ri <= ci)
    strict = (ri > ci, ri < ci)
    last_row = (CHUNK - 1, 0)
    same = lambda sh: (ri >> sh) == (ci >> sh)
    levels = [same(sh + 1) & ~same(sh) for sh in range(N_LEVELS)]

    def total(gc, d):
        return jnp.broadcast_to(gc[last_row[d]:last_row[d] + 1, :], (CHUNK, CHUNK))

    tri = [m.astype(BF16) for m in incl]

    rows = [slice(n * CHUNK, (n + 1) * CHUNK) for n in range(N_CHUNK)]
    chains = [(c, d) for c in range(N_CHUNK) for d in range(2)]
    kk = [_dot(kb_s[r, :], kt_s[n]) for n, r in enumerate(rows)]
    qk = [_dot(qb_s[r, :], kt_s[n]) for n, r in enumerate(rows)]
    recurrence_step()
    gc = {}
    for c, d in chains:
        g = gcol_s[d, rows[c], :]
        hi = g.astype(BF16)
        rem = g - hi.astype(F32)
        mid = rem.astype(BF16)
        lo = (rem - mid.astype(F32)).astype(BF16)
        gc[c, d] = _dot(tri[d], hi) + _dot(tri[d], mid) + _dot(tri[d], lo)
    recurrence_step()
    low, t, a = {}, {}, {}
    for c, d in chains:
        decay = jnp.exp(jnp.where(incl[d], gc[c, d] - gc[c, d].T, -1e30))
        a[c, d] = (qk[c] * decay).astype(BF16)
        low[c, d] = jnp.where(strict[d], kk[c] * bcol_s[d, rows[c], :] * decay, 0.0)
        t[c, d] = eye - jnp.where(levels[0], low[c, d], 0.0)
    recurrence_step()
    for lvl in levels[1:]:
        tb = {k: v.astype(BF16) for k, v in t.items()}
        x = {k: _dot(jnp.where(lvl, low[k], 0.0).astype(BF16), tb[k]).astype(BF16) for k in chains}
        recurrence_step()
        t = {k: t[k] - _dot(tb[k], x[k]) for k in chains}
        recurrence_step()
    uw = {}
    for c, d in chains:
        beta = bcol_s[d, rows[c], :]
        rhs = jnp.concatenate([vf_s[rows[c], :] * beta, kf_s[rows[c], :] * (beta * jnp.exp(gc[c, d]))], axis=1)
        uw[c, d] = _dot(t[c, d].astype(BF16), rhs.astype(BF16))
    au, pw = {}, {}
    for c, d in chains:
        erc = jnp.exp(total(gc[c, d], d) - gc[c, d])
        scaled = jnp.concatenate([uw[c, d][:, :HEAD_W] * erc, uw[c, d][:, HEAD_W:] * erc], axis=1)
        au[c, d] = _dot(a[c, d], uw[c, d].astype(BF16))
        pw[c, d] = _dot(kt_s[c], scaled.astype(BF16))
    while issued[0] < N_CHUNK:
        recurrence_step()
    o = _rms(of_s[rd] + ob_s[rd], nw_ref[...], NORM_EPS) * _silu(z_ref[0])
    o_ref[0] = o.astype(BF16)
    for c, d in chains:
        (of_s, ob_s)[d][wr, rows[c], :] = au[c, d][:, :HEAD_W]
        psi_s[wr, d, rows[c], :] = pw[c, d][:, :HEAD_W]
        m_s[wr, d, c, :CHUNK, :] = pw[c, d][:, HEAD_W:].astype(BF16)
        qe = qb_s[rows[c], :].astype(F32) * jnp.exp(gc[c, d])
        m_s[wr, d, c, CHUNK:, :] = (qe - au[c, d][:, HEAD_W:]).astype(BF16)
        egt_s[wr, d, c] = jnp.exp(total(gc[c, d], d)[:8, :])


def _deltanet(dn, conv_w, gb, norm_w):
    b = dn.shape[0]
    n_items = b * HEADS
    prep_item = lambda s: jnp.minimum(s, n_items - 1)
    scan_item = lambda s: jnp.maximum(s - 1, 0)
    blk = lambda off, item: pl.BlockSpec((1, SEQ, HEAD_W), lambda s: (item(s) // HEADS, 0, off + item(s) % HEADS))
    cw = lambda off: pl.BlockSpec((CONV_K, HEAD_W), lambda s: (0, off + prep_item(s) % HEADS))
    seq_f32 = pltpu.VMEM((SEQ, HEAD_W), F32)
    seq_bf16 = pltpu.VMEM((SEQ, HEAD_W), BF16)
    chunks_bf16 = pltpu.VMEM((N_CHUNK, CHUNK, CHUNK), BF16)
    return pl.pallas_call(
        functools.partial(_dn_kernel, n_items=n_items),
        grid=(n_items + 1,),
        in_specs=[blk(0, prep_item), blk(HEADS, prep_item), blk(2 * HEADS, prep_item), blk(3 * HEADS, scan_item),
                  cw(0), cw(HEADS), cw(2 * HEADS),
                  pl.BlockSpec((1, SEQ, LANES), lambda s: (prep_item(s) // HEADS, 0, 0)), _const_spec((1, HEAD_W))],
        out_specs=blk(0, scan_item),
        out_shape=jax.ShapeDtypeStruct((b, SEQ, MIX_W), BF16),
        scratch_shapes=[pltpu.VMEM((SEQ + 2 * CONV_PAD, HEAD_W), F32),
                        seq_bf16, seq_bf16, chunks_bf16, seq_f32, seq_f32,
                        pltpu.VMEM((2, SEQ, LANES), F32), pltpu.VMEM((2, SEQ, LANES), F32),
                        pltpu.VMEM((2, 2, SEQ, HEAD_W), F32), pltpu.VMEM((2, 2, N_CHUNK, 2 * CHUNK, HEAD_W), BF16),
                        pltpu.VMEM((2, 2, N_CHUNK, 8, LANES), F32),
                        pltpu.VMEM((2, SEQ, HEAD_W), F32), pltpu.VMEM((2, SEQ, HEAD_W), F32)],
        compiler_params=_params(1),
        name="deltanet",
    )(dn, dn, dn, dn, conv_w, conv_w, conv_w, gb, norm_w)


def _da_kernel(q_ref, k_ref, v_ref, tab_ref, lp_ref, nw_ref, o_ref, *, lambda_init):
    lp = lp_ref[...]
    lam = (jnp.exp(jnp.sum(lp[0:1, :] * lp[1:2, :], axis=-1, keepdims=True))
           - jnp.exp(jnp.sum(lp[2:3, :] * lp[3:4, :], axis=-1, keepdims=True)) + lambda_init)
    k = k_ref[0]
    v = v_ref[0]
    lane = lax.broadcasted_iota(jnp.int32, (Q_TILE, HEAD_W), 1)

    def scores(qt, m):
        q = q_ref[0, qt * Q_TILE:(qt + 1) * Q_TILE, :]
        qm = jnp.where((lane >= m * DA_DQK) & (lane < (m + 1) * DA_DQK), q, jnp.zeros_like(q))
        start = (N_QT - 1 - qt) * Q_TILE
        return _dot_nt(qm, k) + tab_ref[0, :, start:start + SEQ]

    units = [(qt, m) for qt in range(N_QT) for m in range(2)]
    ahead = [scores(*u) for u in units[:SCORE_LOOKAHEAD]]
    outs = []
    for i, (qt, m) in enumerate(units):
        s = ahead.pop(0)
        if i + SCORE_LOOKAHEAD < len(units):
            ahead.append(scores(*units[i + SCORE_LOOKAHEAD]))
        p = jnp.exp2(s - jnp.max(s, axis=-1, keepdims=True))
        outs.append(_dot(p.astype(BF16), v) / jnp.sum(p, axis=-1, keepdims=True))
        if m == 1:
            o = outs[0] - lam * outs[1]
            outs = []
            o = _rms(o, nw_ref[...], SUBLN_EPS) * (1.0 - lambda_init)
            o_ref[0, qt * Q_TILE:(qt + 1) * Q_TILE, :] = o.astype(BF16)


def _diff_attention(da, table, lam_params, subln_w, lambda_init):
    b = da.shape[0]
    blk = lambda off: pl.BlockSpec((1, SEQ, HEAD_W), lambda h, i: (i, 0, off + h))
    tab_w = (2 * N_QT - 1) * Q_TILE
    return pl.pallas_call(
        functools.partial(_da_kernel, lambda_init=lambda_init),
        grid=(HEADS, b),
        in_specs=[blk(0), blk(HEADS), blk(2 * HEADS),
                  pl.BlockSpec((1, Q_TILE, tab_w), lambda h, i: (h, 0, 0)),
                  _const_spec((4, DA_DQK)), _const_spec((1, HEAD_W))],
        out_specs=pl.BlockSpec((1, SEQ, HEAD_W), lambda h, i: (i, 0, h)),
        out_shape=jax.ShapeDtypeStruct((b, SEQ, MIX_W), BF16),
        compiler_params=_params(2),
        name="diff_attention",
    )(da, da, da, table, lam_params, subln_w)


def _alibi_table():
    slopes = 2.0 ** (-8.0 * jnp.arange(1, HEADS + 1, dtype=F32) / HEADS)
    i = jnp.arange(Q_TILE, dtype=jnp.int32)[:, None]
    c = jnp.arange((2 * N_QT - 1) * Q_TILE, dtype=jnp.int32)[None, :] - (N_QT - 1) * Q_TILE
    return (-LOG2_E * slopes)[:, None, None] * jnp.abs(i - c).astype(F32)[None]


def _merge_ffn_kernel(*refs, final):
    (x_ref, dn_ref, da_ref, nw_ref, wgate_ref, wbdn_ref, wbda_ref, wout_ref,
     nw2_ref, wg_ref, wu_ref, wd_ref) = refs[:12]
    fw_ref = refs[12] if final else None
    o_ref, h_scr, a_scr = refs[-3:]
    x = x_ref[...]
    h = _rms(x, nw_ref[...], NORM_EPS).astype(BF16)
    gates = jax.nn.sigmoid(_dot(h, wgate_ref[...]))
    y_dn = _dot(dn_ref[...], wbdn_ref[...])
    y_da = _dot(da_ref[...], wbda_ref[...])
    merged = gates[:, :D_MODEL] * y_dn + gates[:, D_MODEL:] * y_da
    x = x + _dot(merged.astype(BF16), wout_ref[...])
    h_scr[...] = _rms(x, nw2_ref[...], NORM_EPS).astype(BF16)
    for j in range(N_FF):
        h = h_scr[...]
        cols = slice(j * FF_TILE, (j + 1) * FF_TILE)
        g = _dot(h, wg_ref[:, cols])
        u = _dot(h, wu_ref[:, cols])
        a_scr[:, cols] = (_silu(g) * u).astype(BF16)
    out = x + 0.5 * _dot(a_scr[...], wd_ref[...])
    if final:
        out = _rms(out, fw_ref[...], NORM_EPS)
    o_ref[...] = out


def _merge_ffn(x, o_dn, o_da, nw, wgate, wbdn, wbda, wout, nw2, wg, wu, wd, final_w=None):
    m = x.shape[0]
    tok = lambda w: pl.BlockSpec((TOK_TILE, w), lambda i: (i, 0))
    in_specs = [tok(D_MODEL), tok(MIX_W), tok(MIX_W), _const_spec((1, D_MODEL)),
                _const_spec((D_MODEL, N_GATE)), _const_spec((MIX_W, D_MODEL)),
                _const_spec((MIX_W, D_MODEL)), _const_spec((D_MODEL, D_MODEL)),
                _const_spec((1, D_MODEL)), _const_spec((D_MODEL, D_FF)), _const_spec((D_MODEL, D_FF)),
                _const_spec((D_FF, D_MODEL))]
    args = [x, o_dn, o_da, nw, wgate, wbdn, wbda, wout, nw2, wg, wu, wd]
    if final_w is not None:
        in_specs.append(_const_spec((1, D_MODEL)))
        args.append(final_w)
    return pl.pallas_call(
        functools.partial(_merge_ffn_kernel, final=final_w is not None),
        grid=(m // TOK_TILE,),
        in_specs=in_specs,
        out_specs=tok(D_MODEL),
        out_shape=jax.ShapeDtypeStruct((m, D_MODEL), F32),
        scratch_shapes=[pltpu.VMEM((TOK_TILE, D_MODEL), BF16), pltpu.VMEM((TOK_TILE, D_FF), BF16)],
        compiler_params=_params(1),
        name="merge_ffn_final" if final_w is not None else "merge_ffn",
    )(*args)


def _lane_row(v):
    return jnp.zeros((1, LANES), F32).at[0, 2 * HEADS:4 * HEADS].set(v.reshape(-1).astype(F32))


def kernel(x_prompt, x_sample, ffn1_norm, ffn1_wg, ffn1_wu, ffn1_wd, mix_norm, w_in, conv_w, dn_a_log, dn_dt_bias, dn_out_norm, diff_lambda, diff_subln, w_branch_dn, w_branch_da, w_out, ffn2_norm, ffn2_wg, ffn2_wu, ffn2_wd, final_norm):
    n_prompt = x_prompt.shape[0]
    batch = n_prompt + x_sample.shape[0]
    x = x_prompt.reshape(n_prompt * SEQ, D_MODEL)
    x2 = x_sample.reshape((batch - n_prompt) * SEQ, D_MODEL)
    table = _alibi_table()
    row = lambda v: v.reshape(1, -1).astype(F32)
    c_dn, c_sm, c_da, c_gate = 4 * MIX_W, 4 * MIX_W + 4 * HEADS, 4 * MIX_W + 4 * HEADS + 3 * MIX_W, w_in.shape[-1]
    for l in range(DEPTH):
        x = _ffn(x, row(ffn1_norm[l]), ffn1_wg[l].astype(BF16), ffn1_wu[l].astype(BF16), ffn1_wd[l].astype(BF16),
                 x2=x2 if l == 0 else None)
        wl = w_in[l]
        wsm = jnp.zeros((D_MODEL, LANES), BF16).at[:, :4 * HEADS].set(wl[:, c_dn:c_sm].astype(BF16))
        dn, da, gb = _inproj(x, row(mix_norm[l]), wl[:, :c_dn].astype(BF16), wl[:, c_sm:c_da].astype(BF16), wsm,
                             _lane_row(dn_a_log[l]), _lane_row(dn_dt_bias[l]))
        o_dn = _deltanet(dn.reshape(batch, SEQ, 4 * MIX_W), conv_w[l].astype(F32),
                         gb.reshape(batch, SEQ, LANES), row(dn_out_norm[l]))
        lambda_init = 0.8 - 0.6 * math.exp(-0.3 * l)
        o_da = _diff_attention(da.reshape(batch, SEQ, 3 * MIX_W), table, diff_lambda[l].astype(F32),
                               row(diff_subln[l]), lambda_init)
        x = _merge_ffn(x, o_dn.reshape(batch * SEQ, MIX_W), o_da.reshape(batch * SEQ, MIX_W), row(mix_norm[l]),
                       wl[:, c_da:c_gate].astype(BF16), w_branch_dn[l].astype(BF16), w_branch_da[l].astype(BF16),
                       w_out[l].astype(BF16), row(ffn2_norm[l]), ffn2_wg[l].astype(BF16), ffn2_wu[l].astype(BF16),
                       ffn2_wd[l].astype(BF16), final_w=row(final_norm) if l == DEPTH - 1 else None)
    y = x.reshape(batch, SEQ, D_MODEL)
    return (y[:n_prompt], y[n_prompt:])
```
